```python
import functools
import jax, jax.numpy as jnp
from jax import lax
import numpy as np

D_MODEL = 1024
BATCH = 8
SEQ = 2048
DEPTH = 1
DEC_BATCH = 128
DEC_SEQ = 1
PAST_LEN = 8192
PAGE_SIZE = 128

HEAD_DIM = 64
FOX_HEADS = 8
SB_HEADS = 8
FOX_WIDTH = FOX_HEADS * HEAD_DIM
SB_WIDTH = SB_HEADS * HEAD_DIM
N_IN = 3 * FOX_WIDTH + FOX_HEADS + 3 * SB_WIDTH + 2 * D_MODEL
BLOCK_Q = 128
PEER_HEADS = 8
PEER_TOPK = 16
N_KEYS = 128
N_EXPERTS = N_KEYS * N_KEYS
D_KEY = 256
PEER_BLOCK = 128
RMS_EPS = 1e-6
FORGET_BIAS_INIT = 3.0

kernel_name = 'fox_stickbreak_peer_adaln_decode_step'


def rmsnorm(x, g):
    xf = x.astype(jnp.float32)
    inv = lax.rsqrt(jnp.mean(xf * xf, axis=-1, keepdims=True) + RMS_EPS)
    return (xf * inv).astype(x.dtype) * g


def modulate(h, shift, scale):
    return h * (1.0 + scale[:, None, :]) + shift[:, None, :]


def mixer_inputs(h, w_in, b_f):
    bsz, s = h.shape[0], h.shape[1]
    proj = jnp.einsum('bsd,de->bse', h, w_in)
    sizes = [FOX_WIDTH] * 3 + [FOX_HEADS] + [SB_WIDTH] * 3 + [D_MODEL, D_MODEL]
    idx = [int(i) for i in np.cumsum(sizes)[:-1]]
    qa, ka, va, fa, qb, kb, vb, ga, gb = jnp.split(proj, idx, axis=-1)
    hd = lambda t, n: t.reshape(bsz, s, n, HEAD_DIM)
    logf = jax.nn.log_sigmoid((fa + b_f).astype(jnp.float32))
    return (hd(qa, FOX_HEADS), hd(ka, FOX_HEADS), hd(va, FOX_HEADS), logf,
            hd(qb, SB_HEADS), hd(kb, SB_HEADS), hd(vb, SB_HEADS),
            jax.nn.sigmoid(ga), jax.nn.sigmoid(gb))


def qk_scores(q, k):
    return jnp.einsum('bqhd,bkhd->bhqk', q, k).astype(jnp.float32) * (HEAD_DIM ** -0.5)


def fox_probs(scores, f_q, f_k, q_off):
    tq, tk = scores.shape[-2], scores.shape[-1]
    bias = jnp.transpose(f_q, (0, 2, 1))[:, :, :, None] - jnp.transpose(f_k, (0, 2, 1))[:, :, None, :]
    causal = jnp.arange(tk)[None, :] <= (q_off + jnp.arange(tq))[:, None]
    return jax.nn.softmax(jnp.where(causal, scores + bias, -jnp.inf), axis=-1)


def sb_weights(scores, q_off):
    tq, tk = scores.shape[-2], scores.shape[-1]
    strict = jnp.arange(tk)[None, :] < (q_off + jnp.arange(tq))[:, None]
    log_keep = jnp.where(strict, jax.nn.log_sigmoid(-scores), 0.0)
    between = lax.cumsum(log_keep, axis=3, reverse=True) - log_keep
    return jnp.where(strict, jnp.exp(jax.nn.log_sigmoid(scores) + between), 0.0)


def prompt_attention(qa, ka, va, logfa, qb, kb, vb):
    s = qa.shape[1]
    f_cum = jnp.cumsum(logfa, axis=1)
    outs_a, outs_b = [], []
    for i in range(s // BLOCK_Q):
        s0, e = i * BLOCK_Q, (i + 1) * BLOCK_Q
        pa = fox_probs(qk_scores(qa[:, s0:e], ka[:, :e]), f_cum[:, s0:e], f_cum[:, :e], s0)
        outs_a.append(jnp.einsum('bhqk,bkhd->bqhd', pa.astype(va.dtype), va[:, :e]))
        wb = sb_weights(qk_scores(qb[:, s0:e], kb[:, :e]), s0)
        outs_b.append(jnp.einsum('bhqk,bkhd->bqhd', wb.astype(vb.dtype), vb[:, :e]))
    return jnp.concatenate(outs_a, axis=1), jnp.concatenate(outs_b, axis=1)


def gather_past(cache, page_table, layer):
    db, n_pages = page_table.shape
    rows = cache[layer, page_table]
    return rows.reshape((db, n_pages * cache.shape[2]) + cache.shape[3:])


def sample_fox(q, k, v, logf, cache_k, cache_v, cache_logf, page_table, layer):
    past_k = gather_past(cache_k, page_table, layer)
    past_v = gather_past(cache_v, page_table, layer)
    past_logf = gather_past(cache_logf, page_table, layer).astype(jnp.float32)
    p_len = past_k.shape[1]
    f_cum = jnp.cumsum(jnp.concatenate([past_logf, logf], axis=1), axis=1)
    scores = jnp.concatenate([qk_scores(q, past_k), qk_scores(q, k)], axis=-1)
    p = fox_probs(scores, f_cum[:, p_len:], f_cum, p_len).astype(v.dtype)
    return (jnp.einsum('bhqk,bkhd->bqhd', p[..., :p_len], past_v)
            + jnp.einsum('bhqk,bkhd->bqhd', p[..., p_len:], v))


def sample_sb(q, k, v, cache_k, cache_v, page_table, layer):
    past_k = gather_past(cache_k, page_table, layer)
    past_v = gather_past(cache_v, page_table, layer)
    p_len = past_k.shape[1]
    scores = jnp.concatenate([qk_scores(q, past_k), qk_scores(q, k)], axis=-1)
    w = sb_weights(scores, p_len).astype(v.dtype)
    return (jnp.einsum('bhqk,bkhd->bqhd', w[..., :p_len], past_v)
            + jnp.einsum('bhqk,bkhd->bqhd', w[..., p_len:], v))


def sample_attention(qa, ka, va, logfa, qb, kb, vb, cache_fox_k, cache_fox_v, cache_fox_logf,
                     cache_sb_k, cache_sb_v, page_table, layer):
    ya = sample_fox(qa, ka, va, logfa, cache_fox_k, cache_fox_v, cache_fox_logf, page_table, layer)
    yb = sample_sb(qb, kb, vb, cache_sb_k, cache_sb_v, page_table, layer)
    return ya, yb


def peer_ffn(h, w_q, sub_keys, u_table, v_table):
    bsz, s, d = h.shape
    t = h.reshape(bsz * s, d)
    n_tok = t.shape[0]
    q = jnp.einsum('td,de->te', t, w_q).reshape(n_tok, PEER_HEADS, 2, D_KEY // 2)
    sc = jnp.einsum('thcd,hckd->thck', q, sub_keys).astype(jnp.float32)
    s1, i1 = lax.top_k(sc[:, :, 0], PEER_TOPK)
    s2, i2 = lax.top_k(sc[:, :, 1], PEER_TOPK)
    cand = (s1[..., :, None] + s2[..., None, :]).reshape(n_tok, PEER_HEADS, PEER_TOPK * PEER_TOPK)
    cand_idx = (i1[..., :, None] * N_KEYS + i2[..., None, :]).reshape(n_tok, PEER_HEADS, PEER_TOPK * PEER_TOPK)
    top_s, pos = lax.top_k(cand, PEER_TOPK)
    experts = jnp.take_along_axis(cand_idx, pos, axis=-1)
    gates = jax.nn.softmax(top_s, axis=-1)
    n_blk = -(-n_tok // PEER_BLOCK)
    pad = n_blk * PEER_BLOCK - n_tok
    t_p = jnp.pad(t, ((0, pad), (0, 0))).reshape(n_blk, PEER_BLOCK, d)
    e_p = jnp.pad(experts, ((0, pad), (0, 0), (0, 0))).reshape(n_blk, PEER_BLOCK, PEER_HEADS, PEER_TOPK)
    g_p = jnp.pad(gates, ((0, pad), (0, 0), (0, 0))).reshape(n_blk, PEER_BLOCK, PEER_HEADS, PEER_TOPK)

    def block(args):
        tb, eb, gb = args
        act = jax.nn.gelu(jnp.einsum('td,thkd->thk', tb, u_table[eb]), approximate=False)
        wts = (gb * act.astype(jnp.float32)).astype(tb.dtype)
        return jnp.einsum('thk,thkd->td', wts, v_table[eb])

    out = lax.map(block, (t_p, e_p, g_p)).reshape(n_blk * PEER_BLOCK, d)[:n_tok]
    return out.reshape(bsz, s, d)


def trunk_layer(x, c, attend, w_ada, b_ada, g_pre_mix, g_post_mix, w_in, b_f, w_o_fox, w_o_sb,
                w_out, g_pre_ffn, g_post_ffn, peer_w_q, peer_sub_keys, peer_u, peer_v):
    bsz, s = x.shape[0], x.shape[1]
    ada = jnp.einsum('bd,de->be', jax.nn.silu(c), w_ada) + b_ada
    shift_m, scale_m, gate_m, shift_f, scale_f, gate_f = jnp.split(ada, 6, axis=-1)
    h = modulate(rmsnorm(x, g_pre_mix), shift_m, scale_m)
    qa, ka, va, logfa, qb, kb, vb, ga, gb = mixer_inputs(h, w_in, b_f)
    ya, yb = attend(qa, ka, va, logfa, qb, kb, vb)
    branch_a = jnp.einsum('bse,ed->bsd', ya.reshape(bsz, s, FOX_WIDTH), w_o_fox)
    branch_b = jnp.einsum('bse,ed->bsd', yb.reshape(bsz, s, SB_WIDTH), w_o_sb)
    mix = jnp.einsum('bse,ed->bsd', ga * branch_a + gb * branch_b, w_out)
    x = x + gate_m[:, None, :] * rmsnorm(mix, g_post_mix)
    h = modulate(rmsnorm(x, g_pre_ffn), shift_f, scale_f)
    x = x + gate_f[:, None, :] * rmsnorm(peer_ffn(h, peer_w_q, peer_sub_keys, peer_u, peer_v), g_post_ffn)
    return x, (ka, va, logfa, kb, vb)


def setup_inputs(seed: int = 0) -> dict:
    key = jax.random.key(seed)
    ks = jax.random.split(key, 28)
    n_pages = PAST_LEN // PAGE_SIZE
    n_used = DEC_BATCH * n_pages
    n_pool = (5 * n_used) // 4 + 1
    f32 = jnp.float32

    def nrm(k, shape, scale):
        return jax.random.normal(k, shape, f32) * scale

    page_table = jax.random.permutation(ks[0], n_pool)[:n_used].reshape(DEC_BATCH, n_pages).astype(jnp.int32)
    kv_a = (DEPTH, n_pool, PAGE_SIZE, FOX_HEADS, HEAD_DIM)
    kv_b = (DEPTH, n_pool, PAGE_SIZE, SB_HEADS, HEAD_DIM)
    d = D_MODEL
    return {
        'x_prompt': nrm(ks[1], (BATCH, SEQ, d), 1.0),
        'x_sample': nrm(ks[2], (DEC_BATCH, DEC_SEQ, d), 1.0),
        'cache_fox_k': nrm(ks[3], kv_a, 1.0),
        'cache_fox_v': nrm(ks[4], kv_a, 1.0),
        'cache_fox_logf': jax.nn.log_sigmoid(nrm(ks[5], (DEPTH, n_pool, PAGE_SIZE, FOX_HEADS), 1.0) + FORGET_BIAS_INIT),
        'cache_sb_k': nrm(ks[6], kv_b, 1.0),
        'cache_sb_v': nrm(ks[7], kv_b, 1.0),
        'page_table': page_table,
        'c_prompt': nrm(ks[8], (BATCH, d), 1.0),
        'c_sample': nrm(ks[9], (DEC_BATCH, d), 1.0),
        'w_ada': nrm(ks[10], (DEPTH, d, 6 * d), 0.5 * d ** -0.5),
        'b_ada': nrm(ks[11], (DEPTH, 6 * d), 0.01),
        'g_pre_mix': 1.0 + nrm(ks[12], (DEPTH, d), 0.1),
        'g_post_mix': 1.0 + nrm(ks[13], (DEPTH, d), 0.1),
        'w_in': nrm(ks[14], (DEPTH, d, N_IN), d ** -0.5),
        'b_f': FORGET_BIAS_INIT + nrm(ks[15], (DEPTH, FOX_HEADS), 0.1),
        'w_o_fox': nrm(ks[16], (DEPTH, FOX_WIDTH, d), FOX_WIDTH ** -0.5),
        'w_o_sb': nrm(ks[17], (DEPTH, SB_WIDTH, d), SB_WIDTH ** -0.5),
        'w_out': nrm(ks[18], (DEPTH, d, d), d ** -0.5),
        'g_pre_ffn': 1.0 + nrm(ks[19], (DEPTH, d), 0.1),
        'g_post_ffn': 1.0 + nrm(ks[20], (DEPTH, d), 0.1),
        'peer_w_q': nrm(ks[21], (DEPTH, d, PEER_HEADS * D_KEY), d ** -0.5),
        'peer_sub_keys': nrm(ks[22], (DEPTH, PEER_HEADS, 2, N_KEYS, D_KEY // 2), (D_KEY // 2) ** -0.5),
        'peer_u': nrm(ks[23], (DEPTH, N_EXPERTS, d), d ** -0.5),
        'peer_v': nrm(ks[24], (DEPTH, N_EXPERTS, d), d ** -0.5),
    }


def reference(x_prompt, x_sample, cache_fox_k, cache_fox_v, cache_fox_logf, cache_sb_k, cache_sb_v,
              page_table, c_prompt, c_sample, w_ada, b_ada, g_pre_mix, g_post_mix, w_in, b_f,
              w_o_fox, w_o_sb, w_out, g_pre_ffn, g_post_ffn, peer_w_q, peer_sub_keys, peer_u, peer_v):
    y_prompt, y_sample = x_prompt, x_sample
    rows_p, rows_s = [], []
    for l in range(DEPTH):
        lw = (w_ada[l], b_ada[l], g_pre_mix[l], g_post_mix[l], w_in[l], b_f[l], w_o_fox[l], w_o_sb[l],
              w_out[l], g_pre_ffn[l], g_post_ffn[l], peer_w_q[l], peer_sub_keys[l], peer_u[l], peer_v[l])
        y_prompt, st_p = trunk_layer(y_prompt, c_prompt, prompt_attention, *lw)
        attend_s = functools.partial(sample_attention, cache_fox_k=cache_fox_k, cache_fox_v=cache_fox_v,
                                     cache_fox_logf=cache_fox_logf, cache_sb_k=cache_sb_k,
                                     cache_sb_v=cache_sb_v, page_table=page_table, layer=l)
        y_sample, st_s = trunk_layer(y_sample, c_sample, attend_s, *lw)
        rows_p.append(st_p)
        rows_s.append(st_s)
    stk = lambda rows, i: jnp.stack([r[i] for r in rows], axis=0)
    return (y_prompt, y_sample,
            stk(rows_p, 0), stk(rows_p, 1), stk(rows_p, 2), stk(rows_p, 3), stk(rows_p, 4),
            stk(rows_s, 0), stk(rows_s, 1), stk(rows_s, 2), stk(rows_s, 3), stk(rows_s, 4))
```

```python
import functools

import jax
import jax.numpy as jnp
from jax import lax
from jax.experimental import pallas as pl
from jax.experimental.pallas import tpu as pltpu

F32 = jnp.float32
BF16 = jnp.bfloat16
I32 = jnp.int32

HEAD_DIM = 64
N_HEADS = 8
WIDTH = N_HEADS * HEAD_DIM
PEER_HEADS = 8
PEER_TOPK = 16
N_KEYS = 128
N_PICKS = PEER_HEADS * PEER_TOPK
RMS_EPS = 1e-6
LANES = 128
VMEM_LIMIT = 56 * 1024 * 1024

_NT = (((1,), (1,)), ((), ()))


def _params(sem):
    return pltpu.CompilerParams(dimension_semantics=sem, vmem_limit_bytes=VMEM_LIMIT)


def _dot(a, b):
    return jnp.dot(a, b, preferred_element_type=F32)


def _dot_nt(a, b):
    return lax.dot_general(a, b, _NT, preferred_element_type=F32)


def _rms(x, g):
    inv = lax.rsqrt(jnp.mean(x * x, axis=-1, keepdims=True) + RMS_EPS)
    return (x * inv) * g


def _log_sigmoid(x):
    return jnp.minimum(x, 0.0) - jnp.log1p(jnp.exp(-jnp.abs(x)))


def _ada_kernel(c_ref, w_ref, b_ref, o_ref):
    c = c_ref[...]
    s = (c * jax.nn.sigmoid(c)).astype(BF16)
    o_ref[...] = _dot(s, w_ref[...]) + b_ref[...]


def _ada(c, w, b):
    rows, d = c.shape
    n = w.shape[1]
    return pl.pallas_call(
        _ada_kernel,
        out_shape=jax.ShapeDtypeStruct((rows, n), F32),
        grid=(n // d,),
        in_specs=[pl.BlockSpec((rows, d), lambda j: (0, 0)),
                  pl.BlockSpec((d, d), lambda j: (0, j)),
                  pl.BlockSpec((1, d), lambda j: (0, j))],
        out_specs=pl.BlockSpec((rows, d), lambda j: (0, j)),
        compiler_params=_params(("arbitrary",)),
        name="ada",
    )(c, w, b)


def _mod_spec(rows, d, chunk, tok_per_group, tm):
    return pl.BlockSpec((None, rows, d), lambda i: ((i * tm) // tok_per_group, 0, chunk))


def _proj_kernel(x_ref, shift_ref, scale_ref, g_ref, wa_ref, wf_ref, bf_ref, wb_ref, wg_ref,
                 qa_ref, ka_ref, va_ref, qb_ref, kb_ref, vb_ref,
                 kaf_ref, vaf_ref, kbf_ref, vbf_ref, lf_ref, ga_ref, gb_ref):
    d = x_ref.shape[1]
    h = _rms(x_ref[...], g_ref[...]) * (1.0 + scale_ref[...]) + shift_ref[...]
    hb = h.astype(BF16)
    scale = HEAD_DIM ** -0.5

    def split(p, q_ref, k_ref, v_ref, kf_ref, vf_ref):
        kf_ref[...] = p[:, WIDTH:2 * WIDTH]
        vf_ref[...] = p[:, 2 * WIDTH:]
        for hh in range(N_HEADS):
            lo = hh * HEAD_DIM
            q_ref[hh] = (p[:, lo:lo + HEAD_DIM] * scale).astype(BF16)
            k_ref[hh] = p[:, WIDTH + lo:WIDTH + lo + HEAD_DIM].astype(BF16)
            v_ref[hh] = p[:, 2 * WIDTH + lo:2 * WIDTH + lo + HEAD_DIM].astype(BF16)

    split(_dot(hb, wa_ref[...]), qa_ref, ka_ref, va_ref, kaf_ref, vaf_ref)
    split(_dot(hb, wb_ref[...]), qb_ref, kb_ref, vb_ref, kbf_ref, vbf_ref)
    lf_ref[...] = _log_sigmoid(_dot(hb, wf_ref[...]) + bf_ref[...])
    ga_ref[...] = jax.nn.sigmoid(_dot(hb, wg_ref[:, :d])).astype(BF16)
    gb_ref[...] = jax.nn.sigmoid(_dot(hb, wg_ref[:, d:])).astype(BF16)


def _proj(x, mod, g, wa, wf, bf, wb, wg, tok_per_group, tm):
    t, d = x.shape
    rows = mod.shape[1]
    full = lambda a: pl.BlockSpec(a.shape, lambda i: (0,) * a.ndim)
    head = jax.ShapeDtypeStruct((N_HEADS, t, HEAD_DIM), BF16)
    flat = jax.ShapeDtypeStruct((t, WIDTH), F32)
    head_spec = pl.BlockSpec((N_HEADS, tm, HEAD_DIM), lambda i: (0, i, 0))
    flat_spec = pl.BlockSpec((tm, WIDTH), lambda i: (i, 0))
    gate = jax.ShapeDtypeStruct((t, d), BF16)
    gate_spec = pl.BlockSpec((tm, d), lambda i: (i, 0))
    return pl.pallas_call(
        _proj_kernel,
        out_shape=(head,) * 6 + (flat,) * 4 + (jax.ShapeDtypeStruct((t, LANES), F32), gate, gate),
        grid=(t // tm,),
        in_specs=[pl.BlockSpec((tm, d), lambda i: (i, 0)),
                  _mod_spec(rows, d, 0, tok_per_group, tm), _mod_spec(rows, d, 1, tok_per_group, tm),
                  full(g), full(wa), full(wf), full(bf), full(wb), full(wg)],
        out_specs=(head_spec,) * 6 + (flat_spec,) * 4
                  + (pl.BlockSpec((tm, LANES), lambda i: (i, 0)), gate_spec, gate_spec),
        compiler_params=_params(("arbitrary",)),
        name="proj",
    )(x, mod, mod, g, wa, wf, bf, wb, wg)


def _fcum_kernel(lf_ref, col_ref, row_ref):
    x = lf_ref[...]
    n = x.shape[0]
    tk = row_ref.shape[1]
    nk = n // tk
    row = lax.broadcasted_iota(I32, x.shape, 0)
    sh = 1
    while sh < n:
        x = x + jnp.where(row >= sh, pltpu.roll(x, sh, axis=0), 0.0)
        sh *= 2
    col_ref[...] = x
    for c in range(n // LANES):
        tr = x[c * LANES:(c + 1) * LANES, :].T
        j, off = divmod(c * LANES, tk)
        for h in range(N_HEADS):
            row_ref[h * nk + j:h * nk + j + 1, off:off + LANES] = tr[h:h + 1, :]


def _fcum(lf, batch, seq, tk):
    nk = seq // tk
    return pl.pallas_call(
        _fcum_kernel,
        out_shape=(jax.ShapeDtypeStruct((batch * seq, LANES), F32),
                   jax.ShapeDtypeStruct((batch, N_HEADS * nk, tk), F32)),
        grid=(batch,),
        in_specs=[pl.BlockSpec((seq, LANES), lambda b: (b, 0))],
        out_specs=(pl.BlockSpec((seq, LANES), lambda b: (b, 0)),
                   pl.BlockSpec((None, N_HEADS * nk, tk), lambda b: (b, 0, 0))),
        compiler_params=_params(("arbitrary",)),
        name="fcum",
    )(lf)


def _prompt_attn_kernel(q_ref, k_ref, v_ref, fcol_ref, frow_ref, o_ref, *, fox, tq):
    seq = q_ref.shape[1]
    nq = seq // tq
    hp = pl.program_id(1)
    rows = lax.broadcasted_iota(I32, (tq, tq), 0)
    cols = lax.broadcasted_iota(I32, (tq, tq), 1)
    later = (rows > cols).astype(BF16)
    lane = lax.broadcasted_iota(I32, (tq, LANES), 1)

    def q_block(i, _):
        r0 = pl.multiple_of(i * tq, tq)
        outs = []
        for hh in range(2):
            head = hp * 2 + hh
            q = q_ref[hh, pl.ds(r0, tq), :]
            if fox:
                fq = jnp.sum(jnp.where(lane == head, fcol_ref[pl.ds(r0, tq), :], 0.0), axis=1, keepdims=True)

                def kv_block(j, carry):
                    m, l, acc = carry
                    c0 = pl.multiple_of(j * tq, tq)
                    k = k_ref[hh, pl.ds(c0, tq), :]
                    v = v_ref[hh, pl.ds(c0, tq), :]
                    fk = frow_ref[pl.ds(head * nq + j, 1), :]
                    s = _dot_nt(q, k) + (fq - fk)
                    s = jnp.where(c0 + cols <= r0 + rows, s, -jnp.inf)
                    m_new = jnp.maximum(m, jnp.max(s, axis=1, keepdims=True))
                    p = jnp.exp(s - m_new)
                    alpha = jnp.exp(m - m_new)
                    l = alpha * l + jnp.sum(p, axis=1, keepdims=True)
                    acc = alpha * acc + _dot(p.astype(BF16), v)
                    return m_new, l, acc

                m0 = jnp.full((tq, 1), -jnp.inf, F32)
                m, l, acc = lax.fori_loop(0, i + 1, kv_block,
                                          (m0, jnp.zeros((tq, 1), F32), jnp.zeros((tq, HEAD_DIM), F32)))
                outs.append(acc / l)
            else:
                def kv_block(jj, carry):
                    run, acc = carry
                    j = i - jj
                    c0 = pl.multiple_of(j * tq, tq)
                    k = k_ref[hh, pl.ds(c0, tq), :]
                    v = v_ref[hh, pl.ds(c0, tq), :]
                    z = _dot_nt(q, k)
                    strict = c0 + cols < r0 + rows
                    lsn = _log_sigmoid(-z)
                    lk = jnp.where(strict, lsn, 0.0)
                    hi = lk.astype(BF16)
                    lo = (lk - hi.astype(F32)).astype(BF16)
                    between = _dot(hi, later) + _dot(lo, later) + run
                    w = jnp.where(strict, jnp.exp(z + lsn + between), 0.0)
                    acc = acc + _dot(w.astype(BF16), v)
                    run = run + jnp.sum(lk, axis=1, keepdims=True)
                    return run, acc

                _, acc = lax.fori_loop(0, i + 1, kv_block,
                                       (jnp.zeros((tq, 1), F32), jnp.zeros((tq, HEAD_DIM), F32)))
                outs.append(acc)
        o_ref[pl.ds(r0, tq), :] = jnp.concatenate(outs, axis=1).astype(o_ref.dtype)
        return 0

    lax.fori_loop(0, nq, q_block, 0)


def _prompt_attn(q, k, v, fcol, frow, batch, seq, fox, tq):
    t = batch * seq
    qkv_spec = pl.BlockSpec((2, seq, HEAD_DIM), lambda b, hp: (hp, b, 0))
    return pl.pallas_call(
        functools.partial(_prompt_attn_kernel, fox=fox, tq=tq),
        out_shape=jax.ShapeDtypeStruct((t, WIDTH), BF16),
        grid=(batch, N_HEADS // 2),
        in_specs=[qkv_spec, qkv_spec, qkv_spec,
                  pl.BlockSpec((seq, LANES), lambda b, hp: (b, 0)),
                  pl.BlockSpec((None,) + frow.shape[1:], lambda b, hp: (b, 0, 0))],
        out_specs=pl.BlockSpec((seq, 2 * HEAD_DIM), lambda b, hp: (b, hp)),
        compiler_params=_params(("arbitrary", "arbitrary")),
        name="prompt_fox" if fox else "prompt_sb",
    )(q, k, v, fcol, frow)


def _suffix_excl(x, lane):
    n = x.shape[1]
    y = x
    d = N_HEADS
    while d < n:
        y = y + jnp.where(lane < n - d, pltpu.roll(y, n - d, axis=1), 0.0)
        d *= 2
    return y - x


def _sample_attn_kernel(pt_ref, kf_ref, vf_ref, lf_ref, kb_ref, vb_ref,
                        qf_ref, qb_ref, knf_ref, vnf_ref, knb_ref, vnb_ref, lfn_ref,
                        ya_ref, yb_ref, m_ref, l_ref, accf_ref, cf_ref, accb_ref, cb_ref):
    p = pl.program_id(1)
    n = kf_ref.shape[0] * kf_ref.shape[1]
    lane = lax.broadcasted_iota(I32, (N_HEADS, n), 1)
    diag = (lane % N_HEADS) == lax.broadcasted_iota(I32, (N_HEADS, n), 0)
    qf = qf_ref[...]
    qb = qb_ref[...]

    @pl.when(p == 0)
    def _():
        knf = knf_ref[...].astype(BF16).astype(F32)
        m_ref[...] = jnp.sum(qf.astype(F32) * knf, axis=1, keepdims=True)
        l_ref[...] = jnp.ones_like(l_ref)
        accf_ref[...] = vnf_ref[...].astype(BF16).astype(F32)
        cf_ref[...] = lfn_ref[...]
        z_self = jnp.sum(qb.astype(F32) * knb_ref[...].astype(BF16).astype(F32), axis=1, keepdims=True)
        strict_self = jnp.zeros_like(z_self) > 0.0
        w_self = jnp.where(strict_self, jnp.exp(_log_sigmoid(z_self)), 0.0)
        accb_ref[...] = w_self * vnb_ref[...].astype(BF16).astype(F32)
        cb_ref[...] = jnp.where(strict_self, _log_sigmoid(-z_self), 0.0)

    k = kf_ref[...].reshape(n, HEAD_DIM).astype(BF16)
    v = vf_ref[...].reshape(n, HEAD_DIM).astype(BF16)
    x = jnp.where(diag, lf_ref[...], 0.0)
    bias = _suffix_excl(x, lane) + cf_ref[...]
    s = jnp.where(diag, _dot_nt(qf, k) + bias, -jnp.inf)
    m_old = m_ref[...]
    m_new = jnp.maximum(m_old, jnp.max(s, axis=1, keepdims=True))
    pr = jnp.exp(s - m_new)
    alpha = jnp.exp(m_old - m_new)
    l_ref[...] = alpha * l_ref[...] + jnp.sum(pr, axis=1, keepdims=True)
    accf_ref[...] = alpha * accf_ref[...] + _dot(pr.astype(BF16), v)
    m_ref[...] = m_new
    cf_ref[...] = cf_ref[...] + jnp.sum(x, axis=1, keepdims=True)

    k = kb_ref[...].reshape(n, HEAD_DIM).astype(BF16)
    v = vb_ref[...].reshape(n, HEAD_DIM).astype(BF16)
    z = _dot_nt(qb, k)
    lsn = _log_sigmoid(-z)
    lk = jnp.where(diag, lsn, 0.0)
    between = _suffix_excl(lk, lane) + cb_ref[...]
    w = jnp.where(diag, jnp.exp(z + lsn + between), 0.0)
    accb_ref[...] = accb_ref[...] + _dot(w.astype(BF16), v)
    cb_ref[...] = cb_ref[...] + jnp.sum(lk, axis=1, keepdims=True)

    @pl.when(p == pl.num_programs(1) - 1)
    def _():
        ya_ref[...] = accf_ref[...] / l_ref[...]
        yb_ref[...] = accb_ref[...]


def _sample_attn(page_table, cfk, cfv, clf, cbk, cbv, qf, qb, knf, vnf, knb, vnb, lfn, layer):
    nb, npg = page_table.shape
    page = cfk.shape[2]
    n = page * N_HEADS

    def page_map(b, p, pt):
        return (layer, pt[b * npg + (npg - 1 - p)], 0, 0, 0)

    kv_spec = pl.BlockSpec((None, None, page, N_HEADS, HEAD_DIM), page_map)
    lf_spec = pl.BlockSpec((None, None, 1, n), lambda b, p, pt: (layer, pt[b * npg + (npg - 1 - p)], 0, 0))
    seq_spec = pl.BlockSpec((None, N_HEADS, HEAD_DIM), lambda b, p, pt: (b, 0, 0))
    col_spec = pl.BlockSpec((None, N_HEADS, 1), lambda b, p, pt: (b, 0, 0))
    out = jax.ShapeDtypeStruct((nb, N_HEADS, HEAD_DIM), F32)
    col = pltpu.VMEM((N_HEADS, 1), F32)
    acc = pltpu.VMEM((N_HEADS, HEAD_DIM), F32)
    return pl.pallas_call(
        _sample_attn_kernel,
        out_shape=(out, out),
        grid_spec=pltpu.PrefetchScalarGridSpec(
            num_scalar_prefetch=1,
            grid=(nb, npg),
            in_specs=[kv_spec, kv_spec, lf_spec, kv_spec, kv_spec,
                      seq_spec, seq_spec, seq_spec, seq_spec, seq_spec, seq_spec, col_spec],
            out_specs=(seq_spec, seq_spec),
            scratch_shapes=[col, col, acc, col, acc, col]),
        compiler_params=_params(("arbitrary", "arbitrary")),
        name="sample_attn",
    )(page_table.reshape(-1), cfk, cfv, clf, cbk, cbv, qf, qb, knf, vnf, knb, vnb, lfn)


def _mixout_kernel(ya_ref, yb_ref, ga_ref, gb_ref, x_ref, gate_ref, shift_ref, scale_ref,
                   wof_ref, wos_ref, wout_ref, gpost_ref, gpre_ref, x1_ref, h2_ref):
    ba = _dot(ya_ref[...], wof_ref[...])
    bb = _dot(yb_ref[...], wos_ref[...])
    mixed = ga_ref[...].astype(F32) * ba + gb_ref[...].astype(F32) * bb
    mix = _dot(mixed.astype(BF16), wout_ref[...])
    x1 = x_ref[...] + gate_ref[...] * _rms(mix, gpost_ref[...])
    x1_ref[...] = x1
    h2_ref[...] = (_rms(x1, gpre_ref[...]) * (1.0 + scale_ref[...]) + shift_ref[...]).astype(BF16)


def _mixout(ya, yb, ga, gb, x, mod, wof, wos, wout, gpost, gpre, tok_per_group, tm):
    t, d = x.shape
    rows = mod.shape[1]
    full = lambda a: pl.BlockSpec(a.shape, lambda i: (0,) * a.ndim)
    tok = lambda w: pl.BlockSpec((tm, w), lambda i: (i, 0))
    return pl.pallas_call(
        _mixout_kernel,
        out_shape=(jax.ShapeDtypeStruct((t, d), F32), jax.ShapeDtypeStruct((t, d), BF16)),
        grid=(t // tm,),
        in_specs=[tok(WIDTH), tok(WIDTH), tok(d), tok(d), tok(d),
                  _mod_spec(rows, d, 2, tok_per_group, tm), _mod_spec(rows, d, 3, tok_per_group, tm),
                  _mod_spec(rows, d, 4, tok_per_group, tm),
                  full(wof), full(wos), full(wout), full(gpost), full(gpre)],
        out_specs=(tok(d), tok(d)),
        compiler_params=_params(("arbitrary",)),
        name="mixout",
    )(ya, yb, ga, gb, x, mod, mod, mod, wof, wos, wout, gpost, gpre)


def _topk_rows(x, k):
    n = x.shape[0]
    row = lax.broadcasted_iota(I32, x.shape, 0)
    vals, idxs = [], []
    for _ in range(k):
        m = jnp.max(x, axis=0, keepdims=True)
        idx = jnp.min(jnp.where(x == m, row, n), axis=0, keepdims=True)
        x = jnp.where(row == idx, -jnp.inf, x)
        vals.append(m)
        idxs.append(idx)
    return jnp.concatenate(vals, axis=0), jnp.concatenate(idxs, axis=0)


def _route_kernel(h_ref, wq_ref, sk_ref, a_ref, b_ref, g_ref, a_scr, b_scr, g_scr):
    h = h_ref[...]

    def head(hd, _):
        def half(c):
            q = _dot(h, wq_ref[2 * hd + c]).astype(BF16)
            return _topk_rows(_dot_nt(sk_ref[2 * hd + c], q), PEER_TOPK)

        s1, i1 = half(0)
        s2, i2 = half(1)
        cand = jnp.concatenate([s1[i:i + 1] + s2 for i in range(PEER_TOPK)], axis=0)
        top_s, pos = _topk_rows(cand, PEER_TOPK)
        pi = pos >> 4
        pj = pos & (PEER_TOPK - 1)
        a = jnp.zeros_like(pos)
        b = jnp.zeros_like(pos)
        for r in range(PEER_TOPK):
            a = jnp.where(pi == r, i1[r:r + 1], a)
            b = jnp.where(pj == r, i2[r:r + 1], b)
        e = jnp.exp(top_s - top_s[0:1])
        gate = e / jnp.sum(e, axis=0, keepdims=True)
        r0 = pl.multiple_of(hd * PEER_TOPK, PEER_TOPK)
        a_scr[pl.ds(r0, PEER_TOPK), :] = a
        b_scr[pl.ds(r0, PEER_TOPK), :] = b
        g_scr[pl.ds(r0, PEER_TOPK), :] = gate
        return 0

    lax.fori_loop(0, PEER_HEADS, head, 0)
    a_ref[...] = a_scr[...].T
    b_ref[...] = b_scr[...].T
    g_ref[...] = g_scr[...].T


def _route(h2, wq, sk, tn):
    t, d = h2.shape
    full = lambda a: pl.BlockSpec(a.shape, lambda i: (0,) * a.ndim)
    spec = pl.BlockSpec((tn, N_PICKS), lambda i: (i, 0))
    return pl.pallas_call(
        _route_kernel,
        out_shape=(jax.ShapeDtypeStruct((t, N_PICKS), I32), jax.ShapeDtypeStruct((t, N_PICKS), I32),
                   jax.ShapeDtypeStruct((t, N_PICKS), F32)),
        grid=(t // tn,),
        in_specs=[pl.BlockSpec((tn, d), lambda i: (i, 0)), full(wq), full(sk)],
        out_specs=(spec, spec, spec),
        scratch_shapes=[pltpu.VMEM((N_PICKS, tn), I32), pltpu.VMEM((N_PICKS, tn), I32),
                        pltpu.VMEM((N_PICKS, tn), F32)],
        compiler_params=_params(("arbitrary",)),
        name="route",
    )(h2, wq, sk)


def _peer_u_kernel(h_ref, u_ref, a_ref, b_ref, o_ref):
    j = pl.program_id(1)
    te = u_ref.shape[0]

    @pl.when(j == 0)
    def _():
        o_ref[...] = jnp.zeros_like(o_ref)

    act = _dot_nt(h_ref[...], u_ref[...])
    a = a_ref[...]
    b = b_ref[...]
    out = o_ref[...]
    for s in range(te // N_KEYS):
        picked = jnp.take_along_axis(act[:, s * N_KEYS:(s + 1) * N_KEYS], b, axis=1)
        out = jnp.where(a == j * (te // N_KEYS) + s, picked, out)
    o_ref[...] = out


def _peer_u(h2, u, a, b, tm, te):
    t, d = h2.shape
    ne = u.shape[0]
    tok = pl.BlockSpec((tm, N_PICKS), lambda i, j: (i, 0))
    return pl.pallas_call(
        _peer_u_kernel,
        out_shape=jax.ShapeDtypeStruct((t, N_PICKS), F32),
        grid=(t // tm, ne // te),
        in_specs=[pl.BlockSpec((tm, d), lambda i, j: (i, 0)), pl.BlockSpec((te, d), lambda i, j: (j, 0)), tok, tok],
        out_specs=tok,
        compiler_params=_params(("arbitrary", "arbitrary")),
        name="peer_u",
    )(h2, u, a, b)


def _peer_v_kernel(a_ref, b_ref, g_ref, act_ref, v_ref, x1_ref, gate_ref, gpost_ref, o_ref,
                   wts_scr, w_scr, acc_ref):
    c = pl.program_id(1)
    tm = a_ref.shape[0]
    tc = v_ref.shape[0]

    @pl.when(c == 0)
    def _():
        act = act_ref[...]
        gelu = 0.5 * act * (1.0 + lax.erf(act * (2.0 ** -0.5)))
        wts_scr[...] = g_ref[...] * gelu
        key = lax.broadcasted_iota(I32, (N_KEYS, N_PICKS), 0)

        def tile(t, _):
            a_row = a_ref[pl.ds(t, 1), :]
            b_row = b_ref[pl.ds(t, 1), :]
            w_row = wts_scr[pl.ds(t, 1), :]
            lhs = jnp.where(key == a_row, w_row, 0.0).astype(BF16)
            rhs = jnp.where(key == b_row, 1.0, 0.0).astype(BF16)
            w_scr[pl.ds(pl.multiple_of(t * N_KEYS, N_KEYS), N_KEYS), :] = _dot_nt(lhs, rhs)
            return 0

        lax.fori_loop(0, tm, tile, 0)
        acc_ref[...] = jnp.zeros_like(acc_ref)

    acc = acc_ref[...]
    for s in range(tc // N_KEYS):
        i1 = c * (tc // N_KEYS) + s
        lhs = w_scr[pl.ds(i1, tm, stride=N_KEYS), :].astype(BF16)
        acc = acc + _dot(lhs, v_ref[s * N_KEYS:(s + 1) * N_KEYS, :])
    acc_ref[...] = acc

    @pl.when(c == pl.num_programs(1) - 1)
    def _():
        o_ref[...] = x1_ref[...] + gate_ref[...] * _rms(acc_ref[...], gpost_ref[...])


def _peer_v(a, b, g, act, v, x1, mod, gpost, tok_per_group, tm, tc):
    t, d = x1.shape
    ne = v.shape[0]
    rows = mod.shape[1]
    tok = lambda w: pl.BlockSpec((tm, w), lambda i, c: (i, 0))
    return pl.pallas_call(
        _peer_v_kernel,
        out_shape=jax.ShapeDtypeStruct((t, d), F32),
        grid=(t // tm, ne // tc),
        in_specs=[tok(N_PICKS), tok(N_PICKS), tok(N_PICKS), tok(N_PICKS),
                  pl.BlockSpec((tc, d), lambda i, c: (c, 0)), tok(d),
                  pl.BlockSpec((None, rows, d), lambda i, c: ((i * tm) // tok_per_group, 0, 5)),
                  pl.BlockSpec(gpost.shape, lambda i, c: (0, 0))],
        out_specs=tok(d),
        scratch_shapes=[pltpu.VMEM((tm, N_PICKS), F32), pltpu.VMEM((tm * N_KEYS, N_KEYS), F32),
                        pltpu.VMEM((tm, d), F32)],
        compiler_params=_params(("arbitrary", "arbitrary")),
        name="peer_v",
    )(a, b, g, act, v, x1, mod, gpost)


def _pick(n, pref):
    for c in pref:
        if n % c == 0:
            return c
    return n


def _trunk(x3, mod, attend, lw, tok_per_group):
    bsz, s, d = x3.shape
    t = bsz * s
    x = x3.reshape(t, d)
    tm = _pick(t, (256, 128))
    (qa, ka, va, qb, kb, vb, kaf, vaf, kbf, vbf, lf, ga, gb) = _proj(
        x, mod, lw["g_pre_mix"], lw["wa"], lw["wf"], lw["bf"], lw["wb"], lw["wg"], tok_per_group, tm)
    ya, yb = attend(qa, ka, va, qb, kb, vb, kaf, vaf, kbf, vbf, lf)
    x1, h2 = _mixout(ya, yb, ga, gb, x, mod, lw["wof"], lw["wos"], lw["wout"], lw["g_post_mix"], lw["g_pre_ffn"],
                     tok_per_group, tm)
    a, b, g = _route(h2, lw["wq"], lw["sk"], _pick(t, (128,)))
    ne = lw["u"].shape[0]
    act = _peer_u(h2, lw["u"], a, b, _pick(t, (512, 256, 128)), _pick(ne, (1024, 512, 256, 128)))
    x2 = _peer_v(a, b, g, act, lw["v"], x1, mod, lw["g_post_ffn"], tok_per_group,
                 _pick(t, (128,)), _pick(ne, (2048, 1024, 512, 256, 128)))
    hd = lambda z: z.reshape(bsz, s, N_HEADS, HEAD_DIM)
    return x2.reshape(bsz, s, d), (hd(kaf), hd(vaf), lf[:, :N_HEADS].reshape(bsz, s, N_HEADS), hd(kbf), hd(vbf))


def kernel(x_prompt, x_sample, cache_fox_k, cache_fox_v, cache_fox_logf, cache_sb_k, cache_sb_v, page_table, c_prompt, c_sample, w_ada, b_ada, g_pre_mix, g_post_mix, w_in, b_f, w_o_fox, w_o_sb, w_out, g_pre_ffn, g_post_ffn, peer_w_q, peer_sub_keys, peer_u, peer_v):
    depth = w_ada.shape[0]
    bsz, seq, d = x_prompt.shape
    nb = x_sample.shape[0]
    assert x_sample.shape[1] == 1 and w_in.shape[2] == 6 * WIDTH + N_HEADS + 2 * d
    assert peer_sub_keys.shape[1:] == (PEER_HEADS, 2, N_KEYS, N_KEYS) and peer_u.shape[1] == N_KEYS * N_KEYS
    page = cache_fox_k.shape[2]
    clf = cache_fox_logf.reshape(depth, cache_fox_logf.shape[1], 1, page * N_HEADS)

    y_p, y_s = x_prompt, x_sample
    rows_p, rows_s = [], []
    for l in range(depth):
        o = 3 * WIDTH
        wf = jnp.zeros((d, LANES), BF16).at[:, :N_HEADS].set(w_in[l][:, o:o + N_HEADS].astype(BF16))
        bf = jnp.zeros((1, LANES), F32).at[0, :N_HEADS].set(b_f[l])
        lw = dict(
            wa=w_in[l][:, :o].astype(BF16), wf=wf, bf=bf,
            wb=w_in[l][:, o + N_HEADS:2 * o + N_HEADS].astype(BF16), wg=w_in[l][:, 2 * o + N_HEADS:].astype(BF16),
            wof=w_o_fox[l].astype(BF16), wos=w_o_sb[l].astype(BF16), wout=w_out[l].astype(BF16),
            g_pre_mix=g_pre_mix[l][None], g_post_mix=g_post_mix[l][None],
            g_pre_ffn=g_pre_ffn[l][None], g_post_ffn=g_post_ffn[l][None],
            wq=peer_w_q[l].reshape(d, 2 * PEER_HEADS, N_KEYS).transpose(1, 0, 2).astype(BF16),
            sk=peer_sub_keys[l].reshape(2 * PEER_HEADS, N_KEYS, N_KEYS).astype(BF16),
            u=peer_u[l].astype(BF16), v=peer_v[l].astype(BF16))

        pad = (-(bsz + nb)) % 16
        c_all = jnp.concatenate([c_prompt, c_sample, jnp.zeros((pad, d), F32)], axis=0)
        ada = _ada(c_all, w_ada[l].astype(BF16), b_ada[l][None])
        mod_p = ada[:bsz].reshape(bsz, 1, 6 * d)
        mod_s = ada[bsz:bsz + nb].reshape(1, nb, 6 * d)

        def attend_p(qa, ka, va, qb, kb, vb, kaf, vaf, kbf, vbf, lf):
            tq = _pick(seq, (256, 128))
            fcol, frow = _fcum(lf, bsz, seq, tq)
            return (_prompt_attn(qa, ka, va, fcol, frow, bsz, seq, True, tq),
                    _prompt_attn(qb, kb, vb, fcol, frow, bsz, seq, False, tq))

        def attend_s(qa, ka, va, qb, kb, vb, kaf, vaf, kbf, vbf, lf, l=l):
            sq = lambda z: z.transpose(1, 0, 2)
            hd = lambda z: z.reshape(nb, N_HEADS, HEAD_DIM)
            ya, yb = _sample_attn(page_table, cache_fox_k, cache_fox_v, clf, cache_sb_k, cache_sb_v,
                                  sq(qa), sq(qb), hd(kaf), hd(vaf), hd(kbf), hd(vbf),
                                  lf[:, :N_HEADS].reshape(nb, N_HEADS, 1), l)
            return ya.reshape(nb, WIDTH).astype(BF16), yb.reshape(nb, WIDTH).astype(BF16)

        y_p, st_p = _trunk(y_p, mod_p, attend_p, lw, seq)
        y_s, st_s = _trunk(y_s, mod_s, attend_s, lw, nb)
        rows_p.append(st_p)
        rows_s.append(st_s)

    stk = lambda rows, i: jnp.stack([r[i] for r in rows], axis=0)
    return (y_p, y_s,
            stk(rows_p, 0), stk(rows_p, 1), stk(rows_p, 2), stk(rows_p, 3), stk(rows_p, 4),
            stk(rows_s, 0), stk(rows_s, 1), stk(rows_s, 2), stk(rows_s, 3), stk(rows_s, 4))
```

```python
import functools

import jax
import jax.numpy as jnp
from jax import lax
from jax.experimental import pallas as pl
from jax.experimental.pallas import tpu as pltpu

F32 = jnp.float32
BF16 = jnp.bfloat16
I32 = jnp.int32

HEAD_DIM = 64
N_HEADS = 8
WIDTH = N_HEADS * HEAD_DIM
PEER_HEADS = 8
PEER_TOPK = 16
N_KEYS = 128
N_PICKS = PEER_HEADS * PEER_TOPK
RMS_EPS = 1e-6
LANES = 128
VMEM_LIMIT = 56 * 1024 * 1024

_NT = (((1,), (1,)), ((), ()))


def _params(sem):
    return pltpu.CompilerParams(dimension_semantics=sem, vmem_limit_bytes=VMEM_LIMIT)


def _dot(a, b):
    return jnp.dot(a, b, preferred_element_type=F32)


def _dot_nt(a, b):
    return lax.dot_general(a, b, _NT, preferred_element_type=F32)


def _rms(x, g):
    inv = lax.rsqrt(jnp.mean(x * x, axis=-1, keepdims=True) + RMS_EPS)
    return (x * inv) * g


def _log_sigmoid(x):
    return jnp.minimum(x, 0.0) - jnp.log(1.0 + jnp.exp(-jnp.abs(x)))


def _ada_kernel(c_ref, w_ref, b_ref, o_ref):
    c = c_ref[...]
    s = (c * jax.nn.sigmoid(c)).astype(BF16)
    o_ref[...] = _dot(s, w_ref[...]) + b_ref[...]


def _ada(c, w, b):
    rows, d = c.shape
    n = w.shape[1]
    return pl.pallas_call(
        _ada_kernel,
        out_shape=jax.ShapeDtypeStruct((rows, n), F32),
        grid=(n // d,),
        in_specs=[pl.BlockSpec((rows, d), lambda j: (0, 0)),
                  pl.BlockSpec((d, d), lambda j: (0, j)),
                  pl.BlockSpec((1, d), lambda j: (0, j))],
        out_specs=pl.BlockSpec((rows, d), lambda j: (0, j)),
        compiler_params=_params(("arbitrary",)),
        name="ada",
    )(c, w, b)


def _mod_spec(rows, d, chunk, tok_per_group, tm):
    return pl.BlockSpec((None, rows, d), lambda i: ((i * tm) // tok_per_group, 0, chunk))


def _proj_kernel(x_ref, shift_ref, scale_ref, g_ref, wa_ref, wf_ref, bf_ref, wb_ref, wg_ref,
                 qa_ref, ka_ref, va_ref, qb_ref, kb_ref, vb_ref,
                 kaf_ref, vaf_ref, kbf_ref, vbf_ref, lf_ref, ga_ref, gb_ref):
    d = x_ref.shape[1]
    h = _rms(x_ref[...], g_ref[...]) * (1.0 + scale_ref[...]) + shift_ref[...]
    hb = h.astype(BF16)
    scale = HEAD_DIM ** -0.5

    def split(p, q_ref, k_ref, v_ref, kf_ref, vf_ref):
        kf_ref[...] = p[:, WIDTH:2 * WIDTH]
        vf_ref[...] = p[:, 2 * WIDTH:]
        for hh in range(N_HEADS):
            lo = hh * HEAD_DIM
            q_ref[hh] = (p[:, lo:lo + HEAD_DIM] * scale).astype(BF16)
            k_ref[hh] = p[:, WIDTH + lo:WIDTH + lo + HEAD_DIM].astype(BF16)
            v_ref[hh] = p[:, 2 * WIDTH + lo:2 * WIDTH + lo + HEAD_DIM].astype(BF16)

    split(_dot(hb, wa_ref[...]), qa_ref, ka_ref, va_ref, kaf_ref, vaf_ref)
    split(_dot(hb, wb_ref[...]), qb_ref, kb_ref, vb_ref, kbf_ref, vbf_ref)
    lf_ref[...] = _log_sigmoid(_dot(hb, wf_ref[...]) + bf_ref[...])
    ga_ref[...] = jax.nn.sigmoid(_dot(hb, wg_ref[:, :d])).astype(BF16)
    gb_ref[...] = jax.nn.sigmoid(_dot(hb, wg_ref[:, d:])).astype(BF16)


def _proj(x, mod, g, wa, wf, bf, wb, wg, tok_per_group, tm):
    t, d = x.shape
    rows = mod.shape[1]
    full = lambda a: pl.BlockSpec(a.shape, lambda i: (0,) * a.ndim)
    head = jax.ShapeDtypeStruct((N_HEADS, t, HEAD_DIM), BF16)
    flat = jax.ShapeDtypeStruct((t, WIDTH), F32)
    head_spec = pl.BlockSpec((N_HEADS, tm, HEAD_DIM), lambda i: (0, i, 0))
    flat_spec = pl.BlockSpec((tm, WIDTH), lambda i: (i, 0))
    gate = jax.ShapeDtypeStruct((t, d), BF16)
    gate_spec = pl.BlockSpec((tm, d), lambda i: (i, 0))
    return pl.pallas_call(
        _proj_kernel,
        out_shape=(head,) * 6 + (flat,) * 4 + (jax.ShapeDtypeStruct((t, LANES), F32), gate, gate),
        grid=(t // tm,),
        in_specs=[pl.BlockSpec((tm, d), lambda i: (i, 0)),
                  _mod_spec(rows, d, 0, tok_per_group, tm), _mod_spec(rows, d, 1, tok_per_group, tm),
                  full(g), full(wa), full(wf), full(bf), full(wb), full(wg)],
        out_specs=(head_spec,) * 6 + (flat_spec,) * 4
                  + (pl.BlockSpec((tm, LANES), lambda i: (i, 0)), gate_spec, gate_spec),
        compiler_params=_params(("arbitrary",)),
        name="proj",
    )(x, mod, mod, g, wa, wf, bf, wb, wg)


def _fcum_kernel(lf_ref, col_ref, row_ref):
    x = lf_ref[...]
    n = x.shape[0]
    tk = row_ref.shape[1]
    nk = n // tk
    row = lax.broadcasted_iota(I32, x.shape, 0)
    sh = 1
    while sh < n:
        x = x + jnp.where(row >= sh, pltpu.roll(x, sh, axis=0), 0.0)
        sh *= 2
    col_ref[...] = x
    for c in range(n // LANES):
        tr = x[c * LANES:(c + 1) * LANES, :].T
        j, off = divmod(c * LANES, tk)
        for h in range(N_HEADS):
            row_ref[h * nk + j:h * nk + j + 1, off:off + LANES] = tr[h:h + 1, :]


def _fcum(lf, batch, seq, tk):
    nk = seq // tk
    return pl.pallas_call(
        _fcum_kernel,
        out_shape=(jax.ShapeDtypeStruct((batch * seq, LANES), F32),
                   jax.ShapeDtypeStruct((batch, N_HEADS * nk, tk), F32)),
        grid=(batch,),
        in_specs=[pl.BlockSpec((seq, LANES), lambda b: (b, 0))],
        out_specs=(pl.BlockSpec((seq, LANES), lambda b: (b, 0)),
                   pl.BlockSpec((None, N_HEADS * nk, tk), lambda b: (b, 0, 0))),
        compiler_params=_params(("arbitrary",)),
        name="fcum",
    )(lf)


def _prompt_attn_kernel(q_ref, k_ref, v_ref, fcol_ref, frow_ref, o_ref, *, fox, tq, hg):
    seq = q_ref.shape[1]
    nq = seq // tq
    h0 = pl.program_id(1) * hg
    rows = lax.broadcasted_iota(I32, (tq, tq), 0)
    cols = lax.broadcasted_iota(I32, (tq, tq), 1)
    causal = cols <= rows
    strict = cols < rows
    later = (rows > cols).astype(BF16)
    lane = lax.broadcasted_iota(I32, (tq, LANES), 1)

    def q_block(i, _):
        r0 = pl.multiple_of(i * tq, tq)
        qs = [q_ref[hh, pl.ds(r0, tq), :] for hh in range(hg)]
        if fox:
            fcol = fcol_ref[pl.ds(r0, tq), :]
            fqs = [jnp.sum(jnp.where(lane == h0 + hh, fcol, 0.0), axis=1, keepdims=True) for hh in range(hg)]

            def step(j, carry, diag):
                c0 = pl.multiple_of(j * tq, tq)
                out = []
                for hh in range(hg):
                    m, l, acc = carry[hh]
                    k = k_ref[hh, pl.ds(c0, tq), :]
                    v = v_ref[hh, pl.ds(c0, tq), :]
                    fk = frow_ref[pl.ds((h0 + hh) * nq + j, 1), :]
                    s = _dot_nt(qs[hh], k) + (fqs[hh] - fk)
                    if diag:
                        s = jnp.where(causal, s, -jnp.inf)
                    m_new = jnp.maximum(m, jnp.max(s, axis=1, keepdims=True))
                    p = jnp.exp(s - m_new)
                    alpha = jnp.exp(m - m_new)
                    l = alpha * l + jnp.sum(p, axis=1, keepdims=True)
                    acc = alpha * acc + _dot(p.astype(BF16), v)
                    out.append((m_new, l, acc))
                return tuple(out)

            init = tuple((jnp.full((tq, 1), -jnp.inf, F32), jnp.zeros((tq, 1), F32),
                          jnp.zeros((tq, HEAD_DIM), F32)) for _ in range(hg))
            carry = lax.fori_loop(0, i, lambda j, c: step(j, c, False), init)
            carry = step(i, carry, True)
            outs = [acc / l for (_, l, acc) in carry]
        else:
            def step(j, carry, diag):
                c0 = pl.multiple_of(j * tq, tq)
                out = []
                for hh in range(hg):
                    run, acc = carry[hh]
                    k = k_ref[hh, pl.ds(c0, tq), :]
                    v = v_ref[hh, pl.ds(c0, tq), :]
                    z = _dot_nt(qs[hh], k)
                    lsn = _log_sigmoid(-z)
                    lk = jnp.where(strict, lsn, 0.0) if diag else lsn
                    hi = lk.astype(BF16)
                    lo = (lk - hi.astype(F32)).astype(BF16)
                    between = _dot(hi, later) + _dot(lo, later) + run
                    w = jnp.exp(z + lsn + between)
                    if diag:
                        w = jnp.where(strict, w, 0.0)
                    acc = acc + _dot(w.astype(BF16), v)
                    run = run + jnp.sum(lk, axis=1, keepdims=True)
                    out.append((run, acc))
                return tuple(out)

            init = tuple((jnp.zeros((tq, 1), F32), jnp.zeros((tq, HEAD_DIM), F32)) for _ in range(hg))
            carry = step(i, init, True)
            carry = lax.fori_loop(0, i, lambda jj, c: step(i - 1 - jj, c, False), carry)
            outs = [acc for (_, acc) in carry]
        o_ref[pl.ds(r0, tq), :] = jnp.concatenate(outs, axis=1).astype(o_ref.dtype)
        return 0

    lax.fori_loop(0, nq, q_block, 0)


def _prompt_attn(q, k, v, fcol, frow, batch, seq, fox, tq, hg):
    t = batch * seq
    qkv_spec = pl.BlockSpec((hg, seq, HEAD_DIM), lambda b, hp: (hp, b, 0))
    return pl.pallas_call(
        functools.partial(_prompt_attn_kernel, fox=fox, tq=tq, hg=hg),
        out_shape=jax.ShapeDtypeStruct((t, WIDTH), BF16),
        grid=(batch, N_HEADS // hg),
        in_specs=[qkv_spec, qkv_spec, qkv_spec,
                  pl.BlockSpec((seq, LANES), lambda b, hp: (b, 0)),
                  pl.BlockSpec((None,) + frow.shape[1:], lambda b, hp: (b, 0, 0))],
        out_specs=pl.BlockSpec((seq, hg * HEAD_DIM), lambda b, hp: (b, hp)),
        compiler_params=_params(("arbitrary", "arbitrary")),
        name="prompt_fox" if fox else "prompt_sb",
    )(q, k, v, fcol, frow)


def _suffix_excl(x, lane):
    n = x.shape[1]
    y = x
    d = 1
    while d < n:
        y = y + jnp.where(lane < n - d, pltpu.roll(y, n - d, axis=1), 0.0)
        d *= 2
    return y - x


def _block_diag(x):
    wide = jnp.concatenate([x] * N_HEADS, axis=1)
    lane = lax.broadcasted_iota(I32, wide.shape, 1)
    row = lax.broadcasted_iota(I32, wide.shape, 0)
    return jnp.where(lane // HEAD_DIM == row, wide, 0.0)


def _diag_blocks(acc):
    row = lax.broadcasted_iota(I32, (N_HEADS, HEAD_DIM), 0)
    out = jnp.zeros((N_HEADS, HEAD_DIM), F32)
    for h in range(N_HEADS):
        out = jnp.where(row == h, acc[:, h * HEAD_DIM:(h + 1) * HEAD_DIM], out)
    return out


def _sample_attn_kernel(pt_ref, *refs, g):
    kf, vf, lf, kb, vb = (refs[i * g:(i + 1) * g] for i in range(5))
    (qf_ref, qb_ref, knf_ref, vnf_ref, knb_ref, vnb_ref, lfn_ref, ya_ref, yb_ref,
     qfd_ref, qbd_ref, m_ref, l_ref, accf_ref, cf_ref, accb_ref, cb_ref) = refs[5 * g:]
    p = pl.program_id(1)

    @pl.when(p == 0)
    def _():
        qf = qf_ref[...]
        qb = qb_ref[...]
        qfd_ref[...] = _block_diag(qf).astype(BF16)
        qbd_ref[...] = _block_diag(qb).astype(BF16)
        m_ref[...] = jnp.sum(qf * knf_ref[...].astype(BF16).astype(F32), axis=1, keepdims=True)
        l_ref[...] = jnp.ones_like(l_ref)
        accf_ref[...] = _block_diag(vnf_ref[...].astype(BF16).astype(F32))
        cf_ref[...] = lfn_ref[...]
        z_self = jnp.sum(qb * knb_ref[...].astype(BF16).astype(F32), axis=1, keepdims=True)
        strict_self = jnp.zeros_like(z_self) > 0.0
        w_self = jnp.where(strict_self, jnp.exp(_log_sigmoid(z_self)), 0.0)
        accb_ref[...] = w_self * _block_diag(vnb_ref[...].astype(BF16).astype(F32))
        cb_ref[...] = jnp.where(strict_self, _log_sigmoid(-z_self), 0.0)

    def pages(rs):
        return jnp.concatenate([r[...].reshape(WIDTH, r.shape[-1]).astype(BF16) for r in rs], axis=1)

    n = g * kf[0].shape[-1]
    lane = lax.broadcasted_iota(I32, (N_HEADS, n), 1)

    x = jnp.concatenate([r[...] for r in lf], axis=1)
    s = _dot(qfd_ref[...], pages(kf)) + (_suffix_excl(x, lane) + cf_ref[...])
    m_old = m_ref[...]
    m_new = jnp.maximum(m_old, jnp.max(s, axis=1, keepdims=True))
    pr = jnp.exp(s - m_new)
    alpha = jnp.exp(m_old - m_new)
    l_ref[...] = alpha * l_ref[...] + jnp.sum(pr, axis=1, keepdims=True)
    accf_ref[...] = alpha * accf_ref[...] + _dot_nt(pr.astype(BF16), pages(vf))
    m_ref[...] = m_new
    cf_ref[...] = cf_ref[...] + jnp.sum(x, axis=1, keepdims=True)

    z = _dot(qbd_ref[...], pages(kb))
    lsn = _log_sigmoid(-z)
    w = jnp.exp(z + lsn + _suffix_excl(lsn, lane) + cb_ref[...])
    accb_ref[...] = accb_ref[...] + _dot_nt(w.astype(BF16), pages(vb))
    cb_ref[...] = cb_ref[...] + jnp.sum(lsn, axis=1, keepdims=True)

    @pl.when(p == pl.num_programs(1) - 1)
    def _():
        ya_ref[...] = _diag_blocks(accf_ref[...] / l_ref[...])
        yb_ref[...] = _diag_blocks(accb_ref[...])


def _sample_attn(page_table, cfk, cfv, clf, cbk, cbv, qf, qb, knf, vnf, knb, vnb, lfn, layer, g):
    nb, npg = page_table.shape
    page = cfk.shape[-1]

    def page_map(gi):
        return lambda b, p, pt: (layer, pt[b * npg + npg - (p + 1) * g + gi], 0, 0, 0)

    def lf_map(gi):
        return lambda b, p, pt: (layer, pt[b * npg + npg - (p + 1) * g + gi], 0, 0)

    kv_specs = [pl.BlockSpec((None, None, N_HEADS, HEAD_DIM, page), page_map(gi)) for gi in range(g)]
    lf_specs = [pl.BlockSpec((None, None, N_HEADS, page), lf_map(gi)) for gi in range(g)]
    seq_spec = pl.BlockSpec((None, N_HEADS, HEAD_DIM), lambda b, p, pt: (b, 0, 0))
    col_spec = pl.BlockSpec((None, N_HEADS, 1), lambda b, p, pt: (b, 0, 0))
    out = jax.ShapeDtypeStruct((nb, N_HEADS, HEAD_DIM), F32)
    col = pltpu.VMEM((N_HEADS, 1), F32)
    acc = pltpu.VMEM((N_HEADS, WIDTH), F32)
    qbd = pltpu.VMEM((N_HEADS, WIDTH), BF16)
    return pl.pallas_call(
        functools.partial(_sample_attn_kernel, g=g),
        out_shape=(out, out),
        grid_spec=pltpu.PrefetchScalarGridSpec(
            num_scalar_prefetch=1,
            grid=(nb, npg // g),
            in_specs=kv_specs + kv_specs + lf_specs + kv_specs + kv_specs
                     + [seq_spec] * 6 + [col_spec],
            out_specs=(seq_spec, seq_spec),
            scratch_shapes=[qbd, qbd, col, col, acc, col, acc, col]),
        compiler_params=_params(("arbitrary", "arbitrary")),
        name="sample_attn",
    )(page_table.reshape(-1), *([cfk] * g), *([cfv] * g), *([clf] * g), *([cbk] * g), *([cbv] * g),
      qf, qb, knf, vnf, knb, vnb, lfn)


def _mixout_kernel(ya_ref, yb_ref, ga_ref, gb_ref, x_ref, gate_ref, shift_ref, scale_ref,
                   wof_ref, wos_ref, wout_ref, gpost_ref, gpre_ref, x1_ref, h2_ref):
    ba = _dot(ya_ref[...], wof_ref[...])
    bb = _dot(yb_ref[...], wos_ref[...])
    mixed = ga_ref[...].astype(F32) * ba + gb_ref[...].astype(F32) * bb
    mix = _dot(mixed.astype(BF16), wout_ref[...])
    x1 = x_ref[...] + gate_ref[...] * _rms(mix, gpost_ref[...])
    x1_ref[...] = x1
    h2_ref[...] = (_rms(x1, gpre_ref[...]) * (1.0 + scale_ref[...]) + shift_ref[...]).astype(BF16)


def _mixout(ya, yb, ga, gb, x, mod, wof, wos, wout, gpost, gpre, tok_per_group, tm):
    t, d = x.shape
    rows = mod.shape[1]
    full = lambda a: pl.BlockSpec(a.shape, lambda i: (0,) * a.ndim)
    tok = lambda w: pl.BlockSpec((tm, w), lambda i: (i, 0))
    return pl.pallas_call(
        _mixout_kernel,
        out_shape=(jax.ShapeDtypeStruct((t, d), F32), jax.ShapeDtypeStruct((t, d), BF16)),
        grid=(t // tm,),
        in_specs=[tok(WIDTH), tok(WIDTH), tok(d), tok(d), tok(d),
                  _mod_spec(rows, d, 2, tok_per_group, tm), _mod_spec(rows, d, 3, tok_per_group, tm),
                  _mod_spec(rows, d, 4, tok_per_group, tm),
                  full(wof), full(wos), full(wout), full(gpost), full(gpre)],
        out_specs=(tok(d), tok(d)),
        compiler_params=_params(("arbitrary",)),
        name="mixout",
    )(ya, yb, ga, gb, x, mod, mod, mod, wof, wos, wout, gpost, gpre)


def _topk_rows(x, pos, k):
    big = jnp.iinfo(jnp.int32).max
    vals, idxs = [], []
    for _ in range(k):
        m = jnp.max(x, axis=0, keepdims=True)
        idx = jnp.min(jnp.where(x == m, pos, big), axis=0, keepdims=True)
        x = jnp.where(pos == idx, -jnp.inf, x)
        vals.append(m)
        idxs.append(idx)
    return jnp.concatenate(vals, axis=0), jnp.concatenate(idxs, axis=0)


def _candidates(s1, s2):
    tn = s1.shape[1]
    sub = lax.broadcasted_iota(I32, (8, tn), 0)
    mid1 = jnp.where(sub < 3, s1[2:3], jnp.where(sub < 5, s1[3:4], s1[4:5]))
    is_j2 = (sub == 0) | (sub == 3) | (sub == 5)
    is_j3 = (sub == 1) | (sub == 4)
    mid2 = jnp.where(is_j2, s2[2:3], jnp.where(is_j3, s2[3:4], s2[4:5]))
    mid_pos = jnp.where(sub < 3, 2 * PEER_TOPK, jnp.where(sub < 5, 3 * PEER_TOPK, 4 * PEER_TOPK)) \
        + jnp.where(is_j2, 2, jnp.where(is_j3, 3, 4))
    neg = -jnp.inf
    groups = [
        (s1[0:1] + s2[0:8], sub),
        (s1[0:1] + s2[8:16], sub + 8),
        (s1[1:2] + s2[0:8], sub + PEER_TOPK),
        (jnp.where(sub >= 2, s1[0:8] + s2[0:1], neg), sub * PEER_TOPK),
        (s1[8:16] + s2[0:1], (sub + 8) * PEER_TOPK),
        (jnp.where(sub >= 2, s1[0:8] + s2[1:2], neg), sub * PEER_TOPK + 1),
        (jnp.where(sub < 6, mid1 + mid2, neg), mid_pos),
    ]
    return (jnp.concatenate([v for v, _ in groups], axis=0), jnp.concatenate([p for _, p in groups], axis=0))


def _route_kernel(h_ref, wq_ref, sk_ref, a_ref, b_ref, g_ref, a_scr, b_scr, g_scr):
    h = h_ref[...]
    key = lax.broadcasted_iota(I32, (N_KEYS, h.shape[0]), 0)

    def head(hd, _):
        def half(c):
            q = _dot(h, wq_ref[2 * hd + c]).astype(BF16)
            return _topk_rows(_dot_nt(sk_ref[2 * hd + c], q), key, PEER_TOPK)

        s1, i1 = half(0)
        s2, i2 = half(1)
        cand, cand_pos = _candidates(s1, s2)
        top_s, pos = _topk_rows(cand, cand_pos, PEER_TOPK)
        pi = pos >> 4
        pj = pos & (PEER_TOPK - 1)
        a = jnp.zeros_like(pos)
        b = jnp.zeros_like(pos)
        for r in range(PEER_TOPK):
            a = jnp.where(pi == r, i1[r:r + 1], a)
            b = jnp.where(pj == r, i2[r:r + 1], b)
        e = jnp.exp(top_s - top_s[0:1])
        gate = e / jnp.sum(e, axis=0, keepdims=True)
        r0 = pl.multiple_of(hd * PEER_TOPK, PEER_TOPK)
        a_scr[pl.ds(r0, PEER_TOPK), :] = a
        b_scr[pl.ds(r0, PEER_TOPK), :] = b
        g_scr[pl.ds(r0, PEER_TOPK), :] = gate
        return 0

    lax.fori_loop(0, PEER_HEADS, head, 0)
    a_ref[...] = a_scr[...].T
    b_ref[...] = b_scr[...].T
    g_ref[...] = g_scr[...].T


def _route(h2, wq, sk, tn):
    t, d = h2.shape
    full = lambda a: pl.BlockSpec(a.shape, lambda i: (0,) * a.ndim)
    spec = pl.BlockSpec((tn, N_PICKS), lambda i: (i, 0))
    return pl.pallas_call(
        _route_kernel,
        out_shape=(jax.ShapeDtypeStruct((t, N_PICKS), I32), jax.ShapeDtypeStruct((t, N_PICKS), I32),
                   jax.ShapeDtypeStruct((t, N_PICKS), F32)),
        grid=(t // tn,),
        in_specs=[pl.BlockSpec((tn, d), lambda i: (i, 0)), full(wq), full(sk)],
        out_specs=(spec, spec, spec),
        scratch_shapes=[pltpu.VMEM((N_PICKS, tn), I32), pltpu.VMEM((N_PICKS, tn), I32),
                        pltpu.VMEM((N_PICKS, tn), F32)],
        compiler_params=_params(("arbitrary",)),
        name="route",
    )(h2, wq, sk)


def _peer_u_kernel(h_ref, u_ref, a_ref, b_ref, o_ref):
    j = pl.program_id(1)
    te = u_ref.shape[0]

    @pl.when(j == 0)
    def _():
        o_ref[...] = jnp.zeros_like(o_ref)

    act = _dot_nt(h_ref[...], u_ref[...])
    a = a_ref[...]
    b = b_ref[...]
    out = o_ref[...]
    for s in range(te // N_KEYS):
        picked = jnp.take_along_axis(act[:, s * N_KEYS:(s + 1) * N_KEYS], b, axis=1)
        out = jnp.where(a == j * (te // N_KEYS) + s, picked, out)
    o_ref[...] = out


def _peer_u(h2, u, a, b, tm, te):
    t, d = h2.shape
    ne = u.shape[0]
    tok = pl.BlockSpec((tm, N_PICKS), lambda i, j: (i, 0))
    return pl.pallas_call(
        _peer_u_kernel,
        out_shape=jax.ShapeDtypeStruct((t, N_PICKS), F32),
        grid=(t // tm, ne // te),
        in_specs=[pl.BlockSpec((tm, d), lambda i, j: (i, 0)), pl.BlockSpec((te, d), lambda i, j: (j, 0)), tok, tok],
        out_specs=tok,
        compiler_params=_params(("arbitrary", "arbitrary")),
        name="peer_u",
    )(h2, u, a, b)


def _peer_v_kernel(a_ref, b_ref, g_ref, act_ref, v_ref, x1_ref, gate_ref, gpost_ref, o_ref,
                   wts_scr, w_scr, acc_ref):
    c = pl.program_id(1)
    tm = a_ref.shape[0]
    tc = v_ref.shape[0]

    @pl.when(c == 0)
    def _():
        act = act_ref[...]
        gelu = 0.5 * act * (1.0 + lax.erf(act * (2.0 ** -0.5)))
        wts_scr[...] = g_ref[...] * gelu
        key = lax.broadcasted_iota(I32, (N_KEYS, N_PICKS), 0)

        def tile(t, _):
            a_row = a_ref[pl.ds(t, 1), :]
            b_row = b_ref[pl.ds(t, 1), :]
            w_row = wts_scr[pl.ds(t, 1), :]
            lhs = jnp.where(key == a_row, w_row, 0.0).astype(BF16)
            rhs = jnp.where(key == b_row, 1.0, 0.0).astype(BF16)
            w_scr[t] = _dot_nt(lhs, rhs)
            return 0

        lax.fori_loop(0, tm, tile, 0, unroll=8)
        acc_ref[...] = jnp.zeros_like(acc_ref)

    tiles = w_scr.reshape(tm * N_KEYS, N_KEYS)

    def rows(i1):
        return tiles[pl.ds(i1, tm, stride=N_KEYS), :]

    acc = acc_ref[...]
    for s in range(tc // (2 * N_KEYS)):
        i1 = c * (tc // N_KEYS) + 2 * s
        lhs = jnp.concatenate([rows(i1), rows(i1 + 1)], axis=1).astype(BF16)
        acc = acc + _dot(lhs, v_ref[2 * s * N_KEYS:2 * (s + 1) * N_KEYS, :])
    acc_ref[...] = acc

    @pl.when(c == pl.num_programs(1) - 1)
    def _():
        o_ref[...] = x1_ref[...] + gate_ref[...] * _rms(acc_ref[...], gpost_ref[...])


def _peer_v(a, b, g, act, v, x1, mod, gpost, tok_per_group, tm, tc):
    t, d = x1.shape
    ne = v.shape[0]
    rows = mod.shape[1]
    tok = lambda w: pl.BlockSpec((tm, w), lambda i, c: (i, 0))
    return pl.pallas_call(
        _peer_v_kernel,
        out_shape=jax.ShapeDtypeStruct((t, d), F32),
        grid=(t // tm, ne // tc),
        in_specs=[tok(N_PICKS), tok(N_PICKS), tok(N_PICKS), tok(N_PICKS),
                  pl.BlockSpec((tc, d), lambda i, c: (c, 0)), tok(d),
                  pl.BlockSpec((None, rows, d), lambda i, c: ((i * tm) // tok_per_group, 0, 5)),
                  pl.BlockSpec(gpost.shape, lambda i, c: (0, 0))],
        out_specs=tok(d),
        scratch_shapes=[pltpu.VMEM((tm, N_PICKS), F32), pltpu.VMEM((tm, N_KEYS, N_KEYS), F32),
                        pltpu.VMEM((tm, d), F32)],
        compiler_params=_params(("arbitrary", "arbitrary")),
        name="peer_v",
    )(a, b, g, act, v, x1, mod, gpost)


def _pick(n, pref):
    for c in pref:
        if n % c == 0:
            return c
    return n


def _trunk(x3, mod, attend, lw, tok_per_group):
    bsz, s, d = x3.shape
    t = bsz * s
    x = x3.reshape(t, d)
    tm = _pick(t, (256, 128))
    (qa, ka, va, qb, kb, vb, kaf, vaf, kbf, vbf, lf, ga, gb) = _proj(
        x, mod, lw["g_pre_mix"], lw["wa"], lw["wf"], lw["bf"], lw["wb"], lw["wg"], tok_per_group, tm)
    ya, yb = attend(qa, ka, va, qb, kb, vb, kaf, vaf, kbf, vbf, lf)
    x1, h2 = _mixout(ya, yb, ga, gb, x, mod, lw["wof"], lw["wos"], lw["wout"], lw["g_post_mix"], lw["g_pre_ffn"],
                     tok_per_group, tm)
    a, b, g = _route(h2, lw["wq"], lw["sk"], _pick(t, (256, 128)))
    ne = lw["u"].shape[0]
    act = _peer_u(h2, lw["u"], a, b, _pick(t, (512, 256, 128)), _pick(ne, (1024, 512, 256, 128)))
    x2 = _peer_v(a, b, g, act, lw["v"], x1, mod, lw["g_post_ffn"], tok_per_group,
                 _pick(t, (256, 128)), _pick(ne, (2048, 1024, 512, 256)))
    hd = lambda z: z.reshape(bsz, s, N_HEADS, HEAD_DIM)
    return x2.reshape(bsz, s, d), (hd(kaf), hd(vaf), lf[:, :N_HEADS].reshape(bsz, s, N_HEADS), hd(kbf), hd(vbf))


def kernel(x_prompt, x_sample, cache_fox_k, cache_fox_v, cache_fox_logf, cache_sb_k, cache_sb_v, page_table, c_prompt, c_sample, w_ada, b_ada, g_pre_mix, g_post_mix, w_in, b_f, w_o_fox, w_o_sb, w_out, g_pre_ffn, g_post_ffn, peer_w_q, peer_sub_keys, peer_u, peer_v):
    depth = w_ada.shape[0]
    bsz, seq, d = x_prompt.shape
    nb = x_sample.shape[0]
    assert x_sample.shape[1] == 1 and w_in.shape[2] == 6 * WIDTH + N_HEADS + 2 * d
    assert peer_sub_keys.shape[1:] == (PEER_HEADS, 2, N_KEYS, N_KEYS) and peer_u.shape[1] == N_KEYS * N_KEYS
    to_pool_order = lambda c: jnp.transpose(c, (0, 1, 3, 4, 2))
    cfk, cfv, cbk, cbv = map(to_pool_order, (cache_fox_k, cache_fox_v, cache_sb_k, cache_sb_v))
    clf = jnp.transpose(cache_fox_logf, (0, 1, 3, 2))
    pages_per_step = _pick(page_table.shape[1], (8, 4, 2))

    y_p, y_s = x_prompt, x_sample
    rows_p, rows_s = [], []
    for l in range(depth):
        o = 3 * WIDTH
        wf = jnp.zeros((d, LANES), BF16).at[:, :N_HEADS].set(w_in[l][:, o:o + N_HEADS].astype(BF16))
        bf = jnp.zeros((1, LANES), F32).at[0, :N_HEADS].set(b_f[l])
        lw = dict(
            wa=w_in[l][:, :o].astype(BF16), wf=wf, bf=bf,
            wb=w_in[l][:, o + N_HEADS:2 * o + N_HEADS].astype(BF16), wg=w_in[l][:, 2 * o + N_HEADS:].astype(BF16),
            wof=w_o_fox[l].astype(BF16), wos=w_o_sb[l].astype(BF16), wout=w_out[l].astype(BF16),
            g_pre_mix=g_pre_mix[l][None], g_post_mix=g_post_mix[l][None],
            g_pre_ffn=g_pre_ffn[l][None], g_post_ffn=g_post_ffn[l][None],
            wq=peer_w_q[l].reshape(d, 2 * PEER_HEADS, N_KEYS).transpose(1, 0, 2).astype(BF16),
            sk=peer_sub_keys[l].reshape(2 * PEER_HEADS, N_KEYS, N_KEYS).astype(BF16),
            u=peer_u[l].astype(BF16), v=peer_v[l].astype(BF16))

        pad = (-(bsz + nb)) % 16
        c_all = jnp.concatenate([c_prompt, c_sample, jnp.zeros((pad, d), F32)], axis=0)
        ada = _ada(c_all, w_ada[l].astype(BF16), b_ada[l][None])
        mod_p = ada[:bsz].reshape(bsz, 1, 6 * d)
        mod_s = ada[bsz:bsz + nb].reshape(1, nb, 6 * d)

        def attend_p(qa, ka, va, qb, kb, vb, kaf, vaf, kbf, vbf, lf):
            tq = _pick(seq, (256, 128))
            fcol, frow = _fcum(lf, bsz, seq, tq)
            hg = N_HEADS // 2
            return (_prompt_attn(qa, ka, va, fcol, frow, bsz, seq, True, tq, hg),
                    _prompt_attn(qb, kb, vb, fcol, frow, bsz, seq, False, tq, hg))

        def attend_s(qa, ka, va, qb, kb, vb, kaf, vaf, kbf, vbf, lf, l=l):
            sq = lambda z: z.transpose(1, 0, 2).astype(F32)
            hd = lambda z: z.reshape(nb, N_HEADS, HEAD_DIM)
            ya, yb = _sample_attn(page_table, cfk, cfv, clf, cbk, cbv,
                                  sq(qa), sq(qb), hd(kaf), hd(vaf), hd(kbf), hd(vbf),
                                  lf[:, :N_HEADS].reshape(nb, N_HEADS, 1), l, pages_per_step)
            return ya.reshape(nb, WIDTH).astype(BF16), yb.reshape(nb, WIDTH).astype(BF16)

        y_p, st_p = _trunk(y_p, mod_p, attend_p, lw, seq)
        y_s, st_s = _trunk(y_s, mod_s, attend_s, lw, nb)
        rows_p.append(st_p)
        rows_s.append(st_s)

    stk = lambda rows, i: jnp.stack([r[i] for r in rows], axis=0)
    return (y_p, y_s,
            stk(rows_p, 0), stk(rows_p, 1), stk(rows_p, 2), stk(rows_p, 3), stk(rows_p, 4),
            stk(rows_s, 0), stk(rows_s, 1), stk(rows_s, 2), stk(rows_s, 3), stk(rows_s, 4))
```

```python
import functools

import jax
import jax.numpy as jnp
from jax import lax
from jax.experimental import pallas as pl
from jax.experimental.pallas import tpu as pltpu

F32 = jnp.float32
BF16 = jnp.bfloat16
I32 = jnp.int32

HEAD_DIM = 64
N_HEADS = 8
WIDTH = N_HEADS * HEAD_DIM
PEER_HEADS = 8
PEER_TOPK = 16
N_KEYS = 128
N_PICKS = PEER_HEADS * PEER_TOPK
RMS_EPS = 1e-6
LANES = 128
VMEM_LIMIT = 56 * 1024 * 1024

_NT = (((1,), (1,)), ((), ()))


def _params(sem):
    return pltpu.CompilerParams(dimension_semantics=sem, vmem_limit_bytes=VMEM_LIMIT)


def _dot(a, b):
    return jnp.dot(a, b, preferred_element_type=F32)


def _dot_nt(a, b):
    return lax.dot_general(a, b, _NT, preferred_element_type=F32)


def _rms(x, g):
    inv = lax.rsqrt(jnp.mean(x * x, axis=-1, keepdims=True) + RMS_EPS)
    return (x * inv) * g


def _log_sigmoid(x):
    return jnp.minimum(x, 0.0) - jnp.log(1.0 + jnp.exp(-jnp.abs(x)))


def _ada_kernel(c_ref, w_ref, b_ref, o_ref):
    c = c_ref[...]
    s = (c * jax.nn.sigmoid(c)).astype(BF16)
    o_ref[...] = _dot(s, w_ref[...]) + b_ref[...]


def _ada(c, w, b):
    rows, d = c.shape
    n = w.shape[1]
    return pl.pallas_call(
        _ada_kernel,
        out_shape=jax.ShapeDtypeStruct((rows, n), F32),
        grid=(n // d,),
        in_specs=[pl.BlockSpec((rows, d), lambda j: (0, 0)),
                  pl.BlockSpec((d, d), lambda j: (0, j)),
                  pl.BlockSpec((1, d), lambda j: (0, j))],
        out_specs=pl.BlockSpec((rows, d), lambda j: (0, j)),
        compiler_params=_params(("arbitrary",)),
        name="ada",
    )(c, w, b)


def _mod_spec(rows, d, chunk, tok_per_group, tm):
    return pl.BlockSpec((None, rows, d), lambda i: ((i * tm) // tok_per_group, 0, chunk))


def _proj_kernel(x_ref, shift_ref, scale_ref, g_ref, wa_ref, wf_ref, bf_ref, wb_ref, wg_ref,
                 qa_ref, ka_ref, va_ref, qb_ref, kb_ref, vb_ref,
                 kaf_ref, vaf_ref, kbf_ref, vbf_ref, lf_ref, ga_ref, gb_ref):
    d = x_ref.shape[1]
    h = _rms(x_ref[...], g_ref[...]) * (1.0 + scale_ref[...]) + shift_ref[...]
    hb = h.astype(BF16)
    scale = HEAD_DIM ** -0.5

    def split(p, q_ref, k_ref, v_ref, kf_ref, vf_ref):
        kf_ref[...] = p[:, WIDTH:2 * WIDTH]
        vf_ref[...] = p[:, 2 * WIDTH:]
        for hh in range(N_HEADS):
            lo = hh * HEAD_DIM
            q_ref[hh] = (p[:, lo:lo + HEAD_DIM] * scale).astype(BF16)
            k_ref[hh] = p[:, WIDTH + lo:WIDTH + lo + HEAD_DIM].astype(BF16)
            v_ref[hh] = p[:, 2 * WIDTH + lo:2 * WIDTH + lo + HEAD_DIM].astype(BF16)

    split(_dot(hb, wa_ref[...]), qa_ref, ka_ref, va_ref, kaf_ref, vaf_ref)
    split(_dot(hb, wb_ref[...]), qb_ref, kb_ref, vb_ref, kbf_ref, vbf_ref)
    lf_ref[...] = _log_sigmoid(_dot(hb, wf_ref[...]) + bf_ref[...])
    ga_ref[...] = jax.nn.sigmoid(_dot(hb, wg_ref[:, :d])).astype(BF16)
    gb_ref[...] = jax.nn.sigmoid(_dot(hb, wg_ref[:, d:])).astype(BF16)


def _proj(x, mod, g, wa, wf, bf, wb, wg, tok_per_group, tm):
    t, d = x.shape
    rows = mod.shape[1]
    full = lambda a: pl.BlockSpec(a.shape, lambda i: (0,) * a.ndim)
    head = jax.ShapeDtypeStruct((N_HEADS, t, HEAD_DIM), BF16)
    flat = jax.ShapeDtypeStruct((t, WIDTH), F32)
    head_spec = pl.BlockSpec((N_HEADS, tm, HEAD_DIM), lambda i: (0, i, 0))
    flat_spec = pl.BlockSpec((tm, WIDTH), lambda i: (i, 0))
    gate = jax.ShapeDtypeStruct((t, d), BF16)
    gate_spec = pl.BlockSpec((tm, d), lambda i: (i, 0))
    return pl.pallas_call(
        _proj_kernel,
        out_shape=(head,) * 6 + (flat,) * 4 + (jax.ShapeDtypeStruct((t, LANES), F32), gate, gate),
        grid=(t // tm,),
        in_specs=[pl.BlockSpec((tm, d), lambda i: (i, 0)),
                  _mod_spec(rows, d, 0, tok_per_group, tm), _mod_spec(rows, d, 1, tok_per_group, tm),
                  full(g), full(wa), full(wf), full(bf), full(wb), full(wg)],
        out_specs=(head_spec,) * 6 + (flat_spec,) * 4
                  + (pl.BlockSpec((tm, LANES), lambda i: (i, 0)), gate_spec, gate_spec),
        compiler_params=_params(("arbitrary",)),
        name="proj",
    )(x, mod, mod, g, wa, wf, bf, wb, wg)


def _fcum_kernel(lf_ref, col_ref, row_ref):
    x = lf_ref[...]
    n = x.shape[0]
    tk = row_ref.shape[1]
    nk = n // tk
    row = lax.broadcasted_iota(I32, x.shape, 0)
    sh = 1
    while sh < n:
        x = x + jnp.where(row >= sh, pltpu.roll(x, sh, axis=0), 0.0)
        sh *= 2
    col_ref[...] = x
    for c in range(n // LANES):
        tr = x[c * LANES:(c + 1) * LANES, :].T
        j, off = divmod(c * LANES, tk)
        for h in range(N_HEADS):
            row_ref[h * nk + j:h * nk + j + 1, off:off + LANES] = tr[h:h + 1, :]


def _fcum(lf, batch, seq, tk):
    nk = seq // tk
    return pl.pallas_call(
        _fcum_kernel,
        out_shape=(jax.ShapeDtypeStruct((batch * seq, LANES), F32),
                   jax.ShapeDtypeStruct((batch, N_HEADS * nk, tk), F32)),
        grid=(batch,),
        in_specs=[pl.BlockSpec((seq, LANES), lambda b: (b, 0))],
        out_specs=(pl.BlockSpec((seq, LANES), lambda b: (b, 0)),
                   pl.BlockSpec((None, N_HEADS * nk, tk), lambda b: (b, 0, 0))),
        compiler_params=_params(("arbitrary",)),
        name="fcum",
    )(lf)


def _suffix_excl(x, later, pieces):
    nh, n = x.shape
    nc = n // LANES
    stack = jnp.concatenate([x[:, c * LANES:(c + 1) * LANES] for c in range(nc)], axis=0)
    terms, rest = [], stack
    for _ in range(pieces):
        terms.append(rest.astype(BF16))
        rest = rest - terms[-1].astype(F32)
    parts = _dot(jnp.concatenate(terms, axis=0), later)
    within = parts[:nc * nh]
    for i in range(1, pieces):
        within = within + parts[i * nc * nh:(i + 1) * nc * nh]
    tot = jnp.sum(stack, axis=1, keepdims=True)
    run = jnp.zeros((nh, 1), F32)
    cols = [None] * nc
    for c in reversed(range(nc)):
        cols[c] = within[c * nh:(c + 1) * nh] + run
        run = run + tot[c * nh:(c + 1) * nh]
    return jnp.concatenate(cols, axis=1), run


def _prompt_attn_kernel(q_ref, k_ref, v_ref, fcol_ref, frow_ref, o_ref, *, fox, tq, hg):
    seq = q_ref.shape[1]
    nq = seq // tq
    h0 = pl.program_id(1) * hg
    rows = lax.broadcasted_iota(I32, (tq, tq), 0)
    cols = lax.broadcasted_iota(I32, (tq, tq), 1)
    causal = cols <= rows
    strict = cols < rows
    later = (lax.broadcasted_iota(I32, (LANES, LANES), 0)
             > lax.broadcasted_iota(I32, (LANES, LANES), 1)).astype(BF16)
    lane = lax.broadcasted_iota(I32, (tq, LANES), 1)

    def q_block(i, _):
        r0 = pl.multiple_of(i * tq, tq)
        qs = [q_ref[hh, pl.ds(r0, tq), :] for hh in range(hg)]
        if fox:
            fcol = fcol_ref[pl.ds(r0, tq), :]
            fqs = [jnp.sum(jnp.where(lane == h0 + hh, fcol, 0.0), axis=1, keepdims=True) for hh in range(hg)]

            def step(j, carry, diag):
                c0 = pl.multiple_of(j * tq, tq)
                out = []
                for hh in range(hg):
                    m, l, acc = carry[hh]
                    k = k_ref[hh, pl.ds(c0, tq), :]
                    v = v_ref[hh, pl.ds(c0, tq), :]
                    fk = frow_ref[pl.ds((h0 + hh) * nq + j, 1), :]
                    s = _dot_nt(qs[hh], k) + (fqs[hh] - fk)
                    if diag:
                        s = jnp.where(causal, s, -jnp.inf)
                    m_new = jnp.maximum(m, jnp.max(s, axis=1, keepdims=True))
                    p = jnp.exp(s - m_new)
                    alpha = jnp.exp(m - m_new)
                    l = alpha * l + jnp.sum(p, axis=1, keepdims=True)
                    acc = alpha * acc + _dot(p.astype(BF16), v)
                    out.append((m_new, l, acc))
                return tuple(out)

            init = tuple((jnp.full((tq, 1), -jnp.inf, F32), jnp.zeros((tq, 1), F32),
                          jnp.zeros((tq, HEAD_DIM), F32)) for _ in range(hg))
            carry = lax.fori_loop(0, i, lambda j, c: step(j, c, False), init)
            carry = step(i, carry, True)
            outs = [acc / l for (_, l, acc) in carry]
        else:
            def step(j, carry, diag):
                c0 = pl.multiple_of(j * tq, tq)
                out = []
                for hh in range(hg):
                    run, acc = carry[hh]
                    k = k_ref[hh, pl.ds(c0, tq), :]
                    v = v_ref[hh, pl.ds(c0, tq), :]
                    z = _dot_nt(qs[hh], k)
                    lsn = _log_sigmoid(-z)
                    lk = jnp.where(strict, lsn, 0.0) if diag else lsn
                    between, keep_sum = _suffix_excl(lk, later, 2)
                    w = jnp.exp(z + lsn + (between + run))
                    if diag:
                        w = jnp.where(strict, w, 0.0)
                    acc = acc + _dot(w.astype(BF16), v)
                    run = run + keep_sum
                    out.append((run, acc))
                return tuple(out)

            init = tuple((jnp.zeros((tq, 1), F32), jnp.zeros((tq, HEAD_DIM), F32)) for _ in range(hg))
            carry = step(i, init, True)
            carry = lax.fori_loop(0, i, lambda jj, c: step(i - 1 - jj, c, False), carry)
            outs = [acc for (_, acc) in carry]
        o_ref[pl.ds(r0, tq), :] = jnp.concatenate(outs, axis=1).astype(o_ref.dtype)
        return 0

    lax.fori_loop(0, nq, q_block, 0)


def _prompt_attn(q, k, v, fcol, frow, batch, seq, fox, tq, hg):
    t = batch * seq
    qkv_spec = pl.BlockSpec((hg, seq, HEAD_DIM), lambda b, hp: (hp, b, 0))
    return pl.pallas_call(
        functools.partial(_prompt_attn_kernel, fox=fox, tq=tq, hg=hg),
        out_shape=jax.ShapeDtypeStruct((t, WIDTH), BF16),
        grid=(batch, N_HEADS // hg),
        in_specs=[qkv_spec, qkv_spec, qkv_spec,
                  pl.BlockSpec((seq, LANES), lambda b, hp: (b, 0)),
                  pl.BlockSpec((None,) + frow.shape[1:], lambda b, hp: (b, 0, 0))],
        out_specs=pl.BlockSpec((seq, hg * HEAD_DIM), lambda b, hp: (b, hp)),
        compiler_params=_params(("arbitrary", "arbitrary")),
        name="prompt_fox" if fox else "prompt_sb",
    )(q, k, v, fcol, frow)


def _block_diag(x):
    wide = jnp.concatenate([x] * N_HEADS, axis=1)
    lane = lax.broadcasted_iota(I32, wide.shape, 1)
    row = lax.broadcasted_iota(I32, wide.shape, 0)
    return jnp.where(lane // HEAD_DIM == row, wide, 0.0)


def _diag_blocks(acc):
    row = lax.broadcasted_iota(I32, (N_HEADS, HEAD_DIM), 0)
    out = jnp.zeros((N_HEADS, HEAD_DIM), F32)
    for h in range(N_HEADS):
        out = jnp.where(row == h, acc[:, h * HEAD_DIM:(h + 1) * HEAD_DIM], out)
    return out


def _sample_attn_kernel(pt_ref, *refs, g):
    kf, vf, lf, kb, vb = (refs[i * g:(i + 1) * g] for i in range(5))
    (qf_ref, qb_ref, knf_ref, vnf_ref, knb_ref, vnb_ref, lfn_ref, ya_ref, yb_ref,
     qfd_ref, qbd_ref, later_ref, m_ref, l_ref, accf_ref, cf_ref, accb_ref, cb_ref) = refs[5 * g:]
    p = pl.program_id(1)

    @pl.when(p == 0)
    def _():
        pos = lax.broadcasted_iota(I32, later_ref.shape, 0)
        later_ref[...] = (pos > lax.broadcasted_iota(I32, later_ref.shape, 1)).astype(BF16)
        qf = qf_ref[...]
        qb = qb_ref[...]
        qfd_ref[...] = _block_diag(qf).astype(BF16)
        qbd_ref[...] = _block_diag(qb).astype(BF16)
        m_ref[...] = jnp.sum(qf * knf_ref[...].astype(BF16).astype(F32), axis=1, keepdims=True)
        l_ref[...] = jnp.ones_like(l_ref)
        accf_ref[...] = _block_diag(vnf_ref[...].astype(BF16).astype(F32))
        cf_ref[...] = lfn_ref[...]
        z_self = jnp.sum(qb * knb_ref[...].astype(BF16).astype(F32), axis=1, keepdims=True)
        strict_self = jnp.zeros_like(z_self) > 0.0
        w_self = jnp.where(strict_self, jnp.exp(_log_sigmoid(z_self)), 0.0)
        accb_ref[...] = w_self * _block_diag(vnb_ref[...].astype(BF16).astype(F32))
        cb_ref[...] = jnp.where(strict_self, _log_sigmoid(-z_self), 0.0)

    def pages(rs):
        return jnp.concatenate([r[...].reshape(WIDTH, r.shape[-1]).astype(BF16) for r in rs], axis=1)

    later = later_ref[...]

    decay, decay_sum = _suffix_excl(jnp.concatenate([r[...] for r in lf], axis=1), later, 3)
    s = _dot(qfd_ref[...], pages(kf)) + (decay + cf_ref[...])
    m_old = m_ref[...]
    m_new = jnp.maximum(m_old, jnp.max(s, axis=1, keepdims=True))
    pr = jnp.exp(s - m_new)
    alpha = jnp.exp(m_old - m_new)
    l_ref[...] = alpha * l_ref[...] + jnp.sum(pr, axis=1, keepdims=True)
    accf_ref[...] = alpha * accf_ref[...] + _dot_nt(pr.astype(BF16), pages(vf))
    m_ref[...] = m_new
    cf_ref[...] = cf_ref[...] + decay_sum

    z = _dot(qbd_ref[...], pages(kb))
    lsn = _log_sigmoid(-z)
    between, keep_sum = _suffix_excl(lsn, later, 3)
    w = jnp.exp(z + lsn + between + cb_ref[...])
    accb_ref[...] = accb_ref[...] + _dot_nt(w.astype(BF16), pages(vb))
    cb_ref[...] = cb_ref[...] + keep_sum

    @pl.when(p == pl.num_programs(1) - 1)
    def _():
        ya_ref[...] = _diag_blocks(accf_ref[...] / l_ref[...])
        yb_ref[...] = _diag_blocks(accb_ref[...])


def _sample_attn(page_table, cfk, cfv, clf, cbk, cbv, qf, qb, knf, vnf, knb, vnb, lfn, layer, g):
    nb, npg = page_table.shape
    page = cfk.shape[-1]

    def page_map(gi):
        return lambda b, p, pt: (layer, pt[b * npg + npg - (p + 1) * g + gi], 0, 0, 0)

    def lf_map(gi):
        return lambda b, p, pt: (layer, pt[b * npg + npg - (p + 1) * g + gi], 0, 0)

    kv_specs = [pl.BlockSpec((None, None, N_HEADS, HEAD_DIM, page), page_map(gi)) for gi in range(g)]
    lf_specs = [pl.BlockSpec((None, None, N_HEADS, page), lf_map(gi)) for gi in range(g)]
    seq_spec = pl.BlockSpec((None, N_HEADS, HEAD_DIM), lambda b, p, pt: (b, 0, 0))
    col_spec = pl.BlockSpec((None, N_HEADS, 1), lambda b, p, pt: (b, 0, 0))
    out = jax.ShapeDtypeStruct((nb, N_HEADS, HEAD_DIM), F32)
    col = pltpu.VMEM((N_HEADS, 1), F32)
    acc = pltpu.VMEM((N_HEADS, WIDTH), F32)
    qbd = pltpu.VMEM((N_HEADS, WIDTH), BF16)
    return pl.pallas_call(
        functools.partial(_sample_attn_kernel, g=g),
        out_shape=(out, out),
        grid_spec=pltpu.PrefetchScalarGridSpec(
            num_scalar_prefetch=1,
            grid=(nb, npg // g),
            in_specs=kv_specs + kv_specs + lf_specs + kv_specs + kv_specs
                     + [seq_spec] * 6 + [col_spec],
            out_specs=(seq_spec, seq_spec),
            scratch_shapes=[qbd, qbd, pltpu.VMEM((LANES, LANES), BF16), col, col, acc, col, acc, col]),
        compiler_params=_params(("arbitrary", "arbitrary")),
        name="sample_attn",
    )(page_table.reshape(-1), *([cfk] * g), *([cfv] * g), *([clf] * g), *([cbk] * g), *([cbv] * g),
      qf, qb, knf, vnf, knb, vnb, lfn)


def _mixout_kernel(ya_ref, yb_ref, ga_ref, gb_ref, x_ref, gate_ref, shift_ref, scale_ref,
                   wof_ref, wos_ref, wout_ref, gpost_ref, gpre_ref, x1_ref, h2_ref):
    ba = _dot(ya_ref[...], wof_ref[...])
    bb = _dot(yb_ref[...], wos_ref[...])
    mixed = ga_ref[...].astype(F32) * ba + gb_ref[...].astype(F32) * bb
    mix = _dot(mixed.astype(BF16), wout_ref[...])
    x1 = x_ref[...] + gate_ref[...] * _rms(mix, gpost_ref[...])
    x1_ref[...] = x1
    h2_ref[...] = (_rms(x1, gpre_ref[...]) * (1.0 + scale_ref[...]) + shift_ref[...]).astype(BF16)


def _mixout(ya, yb, ga, gb, x, mod, wof, wos, wout, gpost, gpre, tok_per_group, tm):
    t, d = x.shape
    rows = mod.shape[1]
    full = lambda a: pl.BlockSpec(a.shape, lambda i: (0,) * a.ndim)
    tok = lambda w: pl.BlockSpec((tm, w), lambda i: (i, 0))
    return pl.pallas_call(
        _mixout_kernel,
        out_shape=(jax.ShapeDtypeStruct((t, d), F32), jax.ShapeDtypeStruct((t, d), BF16)),
        grid=(t // tm,),
        in_specs=[tok(WIDTH), tok(WIDTH), tok(d), tok(d), tok(d),
                  _mod_spec(rows, d, 2, tok_per_group, tm), _mod_spec(rows, d, 3, tok_per_group, tm),
                  _mod_spec(rows, d, 4, tok_per_group, tm),
                  full(wof), full(wos), full(wout), full(gpost), full(gpre)],
        out_specs=(tok(d), tok(d)),
        compiler_params=_params(("arbitrary",)),
        name="mixout",
    )(ya, yb, ga, gb, x, mod, mod, mod, wof, wos, wout, gpost, gpre)


def _topk_rows(x, pos, k):
    big = jnp.iinfo(jnp.int32).max
    vals, idxs = [], []
    for _ in range(k):
        m = jnp.max(x, axis=0, keepdims=True)
        idx = jnp.min(jnp.where(x == m, pos, big), axis=0, keepdims=True)
        x = jnp.where(pos == idx, -jnp.inf, x)
        vals.append(m)
        idxs.append(idx)
    return jnp.concatenate(vals, axis=0), jnp.concatenate(idxs, axis=0)


def _candidates(s1, s2):
    tn = s1.shape[1]
    sub = lax.broadcasted_iota(I32, (8, tn), 0)
    mid1 = jnp.where(sub < 3, s1[2:3], jnp.where(sub < 5, s1[3:4], s1[4:5]))
    is_j2 = (sub == 0) | (sub == 3) | (sub == 5)
    is_j3 = (sub == 1) | (sub == 4)
    mid2 = jnp.where(is_j2, s2[2:3], jnp.where(is_j3, s2[3:4], s2[4:5]))
    mid_pos = jnp.where(sub < 3, 2 * PEER_TOPK, jnp.where(sub < 5, 3 * PEER_TOPK, 4 * PEER_TOPK)) \
        + jnp.where(is_j2, 2, jnp.where(is_j3, 3, 4))
    neg = -jnp.inf
    groups = [
        (s1[0:1] + s2[0:8], sub),
        (s1[0:1] + s2[8:16], sub + 8),
        (s1[1:2] + s2[0:8], sub + PEER_TOPK),
        (jnp.where(sub >= 2, s1[0:8] + s2[0:1], neg), sub * PEER_TOPK),
        (s1[8:16] + s2[0:1], (sub + 8) * PEER_TOPK),
        (jnp.where(sub >= 2, s1[0:8] + s2[1:2], neg), sub * PEER_TOPK + 1),
        (jnp.where(sub < 6, mid1 + mid2, neg), mid_pos),
    ]
    return (jnp.concatenate([v for v, _ in groups], axis=0), jnp.concatenate([p for _, p in groups], axis=0))


def _route_kernel(h_ref, wq_ref, sk_ref, a_ref, b_ref, g_ref, a_scr, b_scr, g_scr):
    h = h_ref[...]
    key = lax.broadcasted_iota(I32, (N_KEYS, h.shape[0]), 0)

    def head(hd, _):
        def half(c):
            q = _dot(h, wq_ref[2 * hd + c]).astype(BF16)
            return _topk_rows(_dot_nt(sk_ref[2 * hd + c], q), key, PEER_TOPK)

        s1, i1 = half(0)
        s2, i2 = half(1)
        cand, cand_pos = _candidates(s1, s2)
        top_s, pos = _topk_rows(cand, cand_pos, PEER_TOPK)
        pi = pos >> 4
        pj = pos & (PEER_TOPK - 1)
        a = jnp.zeros_like(pos)
        b = jnp.zeros_like(pos)
        for r in range(PEER_TOPK):
            a = jnp.where(pi == r, i1[r:r + 1], a)
            b = jnp.where(pj == r, i2[r:r + 1], b)
        e = jnp.exp(top_s - top_s[0:1])
        gate = e / jnp.sum(e, axis=0, keepdims=True)
        r0 = pl.multiple_of(hd * PEER_TOPK, PEER_TOPK)
        a_scr[pl.ds(r0, PEER_TOPK), :] = a
        b_scr[pl.ds(r0, PEER_TOPK), :] = b
        g_scr[pl.ds(r0, PEER_TOPK), :] = gate
        return 0

    lax.fori_loop(0, PEER_HEADS, head, 0)
    a_ref[...] = a_scr[...].T
    b_ref[...] = b_scr[...].T
    g_ref[...] = g_scr[...].T


def _route(h2, wq, sk, tn):
    t, d = h2.shape
    full = lambda a: pl.BlockSpec(a.shape, lambda i: (0,) * a.ndim)
    spec = pl.BlockSpec((tn, N_PICKS), lambda i: (i, 0))
    return pl.pallas_call(
        _route_kernel,
        out_shape=(jax.ShapeDtypeStruct((t, N_PICKS), I32), jax.ShapeDtypeStruct((t, N_PICKS), I32),
                   jax.ShapeDtypeStruct((t, N_PICKS), F32)),
        grid=(t // tn,),
        in_specs=[pl.BlockSpec((tn, d), lambda i: (i, 0)), full(wq), full(sk)],
        out_specs=(spec, spec, spec),
        scratch_shapes=[pltpu.VMEM((N_PICKS, tn), I32), pltpu.VMEM((N_PICKS, tn), I32),
                        pltpu.VMEM((N_PICKS, tn), F32)],
        compiler_params=_params(("arbitrary",)),
        name="route",
    )(h2, wq, sk)


def _peer_u_kernel(h_ref, u_ref, a_ref, b_ref, o_ref):
    j = pl.program_id(1)
    te = u_ref.shape[0]

    @pl.when(j == 0)
    def _():
        o_ref[...] = jnp.zeros_like(o_ref)

    act = _dot_nt(h_ref[...], u_ref[...])
    a = a_ref[...]
    b = b_ref[...]
    out = o_ref[...]
    for s in range(te // N_KEYS):
        picked = jnp.take_along_axis(act[:, s * N_KEYS:(s + 1) * N_KEYS], b, axis=1)
        out = jnp.where(a == j * (te // N_KEYS) + s, picked, out)
    o_ref[...] = out


def _peer_u(h2, u, a, b, tm, te):
    t, d = h2.shape
    ne = u.shape[0]
    tok = pl.BlockSpec((tm, N_PICKS), lambda i, j: (i, 0))
    return pl.pallas_call(
        _peer_u_kernel,
        out_shape=jax.ShapeDtypeStruct((t, N_PICKS), F32),
        grid=(t // tm, ne // te),
        in_specs=[pl.BlockSpec((tm, d), lambda i, j: (i, 0)), pl.BlockSpec((te, d), lambda i, j: (j, 0)), tok, tok],
        out_specs=tok,
        compiler_params=_params(("arbitrary", "arbitrary")),
        name="peer_u",
    )(h2, u, a, b)


def _peer_v_kernel(a_ref, b_ref, g_ref, act_ref, v_ref, x1_ref, gate_ref, gpost_ref, o_ref,
                   wts_scr, w_scr, acc_ref):
    c = pl.program_id(1)
    tm = a_ref.shape[0]
    tc = v_ref.shape[0]

    @pl.when(c == 0)
    def _():
        act = act_ref[...]
        gelu = 0.5 * act * (1.0 + lax.erf(act * (2.0 ** -0.5)))
        wts_scr[...] = g_ref[...] * gelu
        key = lax.broadcasted_iota(I32, (N_KEYS, N_PICKS), 0)

        def tile(t, _):
            a_row = a_ref[pl.ds(t, 1), :]
            b_row = b_ref[pl.ds(t, 1), :]
            w_row = wts_scr[pl.ds(t, 1), :]
            lhs = jnp.where(key == a_row, w_row, 0.0).astype(BF16)
            rhs = jnp.where(key == b_row, 1.0, 0.0).astype(BF16)
            w_scr[t] = _dot_nt(lhs, rhs)
            return 0

        lax.fori_loop(0, tm, tile, 0, unroll=32)
        acc_ref[...] = jnp.zeros_like(acc_ref)

    tiles = w_scr.reshape(tm * N_KEYS, N_KEYS)

    def rows(i1):
        return tiles[pl.ds(i1, tm, stride=N_KEYS), :]

    acc = acc_ref[...]
    for s in range(tc // (2 * N_KEYS)):
        i1 = c * (tc // N_KEYS) + 2 * s
        lhs = jnp.concatenate([rows(i1), rows(i1 + 1)], axis=1).astype(BF16)
        acc = acc + _dot(lhs, v_ref[2 * s * N_KEYS:2 * (s + 1) * N_KEYS, :])
    acc_ref[...] = acc

    @pl.when(c == pl.num_programs(1) - 1)
    def _():
        o_ref[...] = x1_ref[...] + gate_ref[...] * _rms(acc_ref[...], gpost_ref[...])


def _peer_v(a, b, g, act, v, x1, mod, gpost, tok_per_group, tm, tc):
    t, d = x1.shape
    ne = v.shape[0]
    rows = mod.shape[1]
    tok = lambda w: pl.BlockSpec((tm, w), lambda i, c: (i, 0))
    return pl.pallas_call(
        _peer_v_kernel,
        out_shape=jax.ShapeDtypeStruct((t, d), F32),
        grid=(t // tm, ne // tc),
        in_specs=[tok(N_PICKS), tok(N_PICKS), tok(N_PICKS), tok(N_PICKS),
                  pl.BlockSpec((tc, d), lambda i, c: (c, 0)), tok(d),
                  pl.BlockSpec((None, rows, d), lambda i, c: ((i * tm) // tok_per_group, 0, 5)),
                  pl.BlockSpec(gpost.shape, lambda i, c: (0, 0))],
        out_specs=tok(d),
        scratch_shapes=[pltpu.VMEM((tm, N_PICKS), F32), pltpu.VMEM((tm, N_KEYS, N_KEYS), F32),
                        pltpu.VMEM((tm, d), F32)],
        compiler_params=_params(("arbitrary", "arbitrary")),
        name="peer_v",
    )(a, b, g, act, v, x1, mod, gpost)


def _pick(n, pref):
    for c in pref:
        if n % c == 0:
            return c
    return n


def _trunk(x3, mod, attend, lw, tok_per_group):
    bsz, s, d = x3.shape
    t = bsz * s
    x = x3.reshape(t, d)
    tm = _pick(t, (256, 128))
    (qa, ka, va, qb, kb, vb, kaf, vaf, kbf, vbf, lf, ga, gb) = _proj(
        x, mod, lw["g_pre_mix"], lw["wa"], lw["wf"], lw["bf"], lw["wb"], lw["wg"], tok_per_group, tm)
    ya, yb = attend(qa, ka, va, qb, kb, vb, kaf, vaf, kbf, vbf, lf)
    x1, h2 = _mixout(ya, yb, ga, gb, x, mod, lw["wof"], lw["wos"], lw["wout"], lw["g_post_mix"], lw["g_pre_ffn"],
                     tok_per_group, tm)
    a, b, g = _route(h2, lw["wq"], lw["sk"], _pick(t, (256, 128)))
    ne = lw["u"].shape[0]
    act = _peer_u(h2, lw["u"], a, b, _pick(t, (512, 256, 128)), _pick(ne, (4096, 2048, 1024, 512, 256, 128)))
    x2 = _peer_v(a, b, g, act, lw["v"], x1, mod, lw["g_post_ffn"], tok_per_group,
                 _pick(t, (256, 128)), _pick(ne, (2048, 1024, 512, 256)))
    hd = lambda z: z.reshape(bsz, s, N_HEADS, HEAD_DIM)
    return x2.reshape(bsz, s, d), (hd(kaf), hd(vaf), lf[:, :N_HEADS].reshape(bsz, s, N_HEADS), hd(kbf), hd(vbf))


def kernel(x_prompt, x_sample, cache_fox_k, cache_fox_v, cache_fox_logf, cache_sb_k, cache_sb_v, page_table, c_prompt, c_sample, w_ada, b_ada, g_pre_mix, g_post_mix, w_in, b_f, w_o_fox, w_o_sb, w_out, g_pre_ffn, g_post_ffn, peer_w_q, peer_sub_keys, peer_u, peer_v):
    depth = w_ada.shape[0]
    bsz, seq, d = x_prompt.shape
    nb = x_sample.shape[0]
    assert x_sample.shape[1] == 1 and w_in.shape[2] == 6 * WIDTH + N_HEADS + 2 * d
    assert peer_sub_keys.shape[1:] == (PEER_HEADS, 2, N_KEYS, N_KEYS) and peer_u.shape[1] == N_KEYS * N_KEYS
    to_pool_order = lambda c: jnp.transpose(c, (0, 1, 3, 4, 2))
    cfk, cfv, cbk, cbv = map(to_pool_order, (cache_fox_k, cache_fox_v, cache_sb_k, cache_sb_v))
    clf = jnp.transpose(cache_fox_logf, (0, 1, 3, 2))
    pages_per_step = _pick(page_table.shape[1], (8, 4, 2))

    y_p, y_s = x_prompt, x_sample
    rows_p, rows_s = [], []
    for l in range(depth):
        o = 3 * WIDTH
        wf = jnp.zeros((d, LANES), BF16).at[:, :N_HEADS].set(w_in[l][:, o:o + N_HEADS].astype(BF16))
        bf = jnp.zeros((1, LANES), F32).at[0, :N_HEADS].set(b_f[l])
        lw = dict(
            wa=w_in[l][:, :o].astype(BF16), wf=wf, bf=bf,
            wb=w_in[l][:, o + N_HEADS:2 * o + N_HEADS].astype(BF16), wg=w_in[l][:, 2 * o + N_HEADS:].astype(BF16),
            wof=w_o_fox[l].astype(BF16), wos=w_o_sb[l].astype(BF16), wout=w_out[l].astype(BF16),
            g_pre_mix=g_pre_mix[l][None], g_post_mix=g_post_mix[l][None],
            g_pre_ffn=g_pre_ffn[l][None], g_post_ffn=g_post_ffn[l][None],
            wq=peer_w_q[l].reshape(d, 2 * PEER_HEADS, N_KEYS).transpose(1, 0, 2).astype(BF16),
            sk=peer_sub_keys[l].reshape(2 * PEER_HEADS, N_KEYS, N_KEYS).astype(BF16),
            u=peer_u[l].astype(BF16), v=peer_v[l].astype(BF16))

        pad = (-(bsz + nb)) % 16
        c_all = jnp.concatenate([c_prompt, c_sample, jnp.zeros((pad, d), F32)], axis=0)
        ada = _ada(c_all, w_ada[l].astype(BF16), b_ada[l][None])
        mod_p = ada[:bsz].reshape(bsz, 1, 6 * d)
        mod_s = ada[bsz:bsz + nb].reshape(1, nb, 6 * d)

        def attend_p(qa, ka, va, qb, kb, vb, kaf, vaf, kbf, vbf, lf):
            tq = _pick(seq, (512, 256, 128))
            fcol, frow = _fcum(lf, bsz, seq, tq)
            hg = 2
            return (_prompt_attn(qa, ka, va, fcol, frow, bsz, seq, True, tq, hg),
                    _prompt_attn(qb, kb, vb, fcol, frow, bsz, seq, False, tq, hg))

        def attend_s(qa, ka, va, qb, kb, vb, kaf, vaf, kbf, vbf, lf, l=l):
            sq = lambda z: z.transpose(1, 0, 2).astype(F32)
            hd = lambda z: z.reshape(nb, N_HEADS, HEAD_DIM)
            ya, yb = _sample_attn(page_table, cfk, cfv, clf, cbk, cbv,
                                  sq(qa), sq(qb), hd(kaf), hd(vaf), hd(kbf), hd(vbf),
                                  lf[:, :N_HEADS].reshape(nb, N_HEADS, 1), l, pages_per_step)
            return ya.reshape(nb, WIDTH).astype(BF16), yb.reshape(nb, WIDTH).astype(BF16)

        y_p, st_p = _trunk(y_p, mod_p, attend_p, lw, seq)
        y_s, st_s = _trunk(y_s, mod_s, attend_s, lw, nb)
        rows_p.append(st_p)
        rows_s.append(st_s)

    stk = lambda rows, i: jnp.stack([r[i] for r in rows], axis=0)
    return (y_p, y_s,
            stk(rows_p, 0), stk(rows_p, 1), stk(rows_p, 2), stk(rows_p, 3), stk(rows_p, 4),
            stk(rows_s, 0), stk(rows_s, 1), stk(rows_s, 2), stk(rows_s, 3), stk(rows_s, 4))
```
